```python
import math
import jax
import jax.numpy as jnp
from jax import lax
import numpy as np

D_MODEL = 4096
BATCH = 4
SEQ = 4096
DEPTH = 1

N_MEM = 256
MIX_WIDTH = D_MODEL
ATTN_WIDTH = MIX_WIDTH // 2
CONV_WIDTH = MIX_WIDTH - ATTN_WIDTH
DIFF_HEAD_DIM = 128
N_DIFF_HEADS = ATTN_WIDTH // (2 * DIFF_HEAD_DIM)
IN_WIDTH = 3 * ATTN_WIDTH + 2 * CONV_WIDTH
CONV_KERNEL = 31
N_BUCKETS = 32
MAX_DISTANCE = 128
Q_BLOCK = 128
N_CROSS_HEADS = 4
CROSS_HEAD_DIM = D_MODEL // 16
CROSS_WIDTH = N_CROSS_HEADS * CROSS_HEAD_DIM
N_EXPERTS = 32
TOP_K = 4
D_FF_EXPERT = 3 * D_MODEL // 8
SWIGLU_LIMIT = 7.0
SWIGLU_ALPHA = 1.702
MOE_BLOCK = 128
EPS = 1e-6
F32 = jnp.float32

kernel_name = 'hybrid_diffattn_conformer_moe_encoder'


def rms_norm(x, g):
    xf = x.astype(F32)
    y = xf * lax.rsqrt(jnp.mean(xf * xf, axis=-1, keepdims=True) + EPS)
    return (y * g.astype(F32)).astype(x.dtype)


def layer_norm(x, g, b):
    xf = x.astype(F32)
    xc = xf - jnp.mean(xf, axis=-1, keepdims=True)
    y = xc * lax.rsqrt(jnp.mean(xc * xc, axis=-1, keepdims=True) + EPS)
    return (y * g.astype(F32) + b.astype(F32)).astype(x.dtype)


def t5_bucket(rel):
    half = N_BUCKETS // 2
    max_exact = half // 2
    ret = (rel > 0).astype(jnp.int32) * half
    n = jnp.abs(rel)
    nf = jnp.maximum(n, 1).astype(F32)
    large = max_exact + (jnp.log(nf / max_exact) / math.log(MAX_DISTANCE / max_exact)
                         * (half - max_exact)).astype(jnp.int32)
    large = jnp.minimum(large, half - 1)
    return ret + jnp.where(n < max_exact, n, large)


def diff_attention_group(u_attn, lambda_q1, lambda_k1, lambda_q2, lambda_k2,
                         g_subln, rel_bias, lambda_init):
    B, S, _ = u_attn.shape
    H, DH = N_DIFF_HEADS, DIFF_HEAD_DIM
    q, k, v = jnp.split(u_attn, 3, axis=-1)
    q = q.reshape(B, S, H, 2, DH).transpose(3, 0, 2, 1, 4)
    k = k.reshape(B, S, H, 2, DH).transpose(3, 0, 2, 1, 4)
    v = v.reshape(B, S, H, 2 * DH).transpose(0, 2, 1, 3)
    lam = (jnp.exp(jnp.sum(lambda_q1.astype(F32) * lambda_k1.astype(F32)))
           - jnp.exp(jnp.sum(lambda_q2.astype(F32) * lambda_k2.astype(F32)))
           + lambda_init)
    scale = DH ** -0.5
    k_pos = jnp.arange(S, dtype=jnp.int32)

    def query_block(i):
        q0 = i * Q_BLOCK
        qb = lax.dynamic_slice_in_dim(q, q0, Q_BLOCK, axis=3)
        q_pos = q0 + jnp.arange(Q_BLOCK, dtype=jnp.int32)
        bucket = t5_bucket(k_pos[None, :] - q_pos[:, None])
        bias = jnp.transpose(rel_bias[bucket], (2, 0, 1)).astype(F32)
        s = jnp.einsum('mbhqd,mbhkd->mbhqk', qb, k).astype(F32) * scale + bias
        p = jax.nn.softmax(s, axis=-1)
        p = p[0] - lam * p[1]
        return jnp.einsum('bhqk,bhkd->bhqd', p.astype(v.dtype), v)

    o = lax.map(query_block, jnp.arange(S // Q_BLOCK, dtype=jnp.int32))
    o = o.transpose(1, 2, 0, 3, 4).reshape(B, H, S, 2 * DH)
    o = rms_norm(o, g_subln) * (1.0 - lambda_init)
    return o.transpose(0, 2, 1, 3).reshape(B, S, ATTN_WIDTH)


def conformer_conv_group(u_conv, w_dw, b_dw, g_ln, b_ln):
    a, gate = jnp.split(u_conv, 2, axis=-1)
    z = a * jax.nn.sigmoid(gate)
    pad = CONV_KERNEL // 2
    z = lax.conv_general_dilated(
        z, w_dw[:, None, :], window_strides=(1,), padding=[(pad, pad)],
        dimension_numbers=('NWC', 'WIO', 'NWC'),
        feature_group_count=CONV_WIDTH) + b_dw
    z = layer_norm(z, g_ln, b_ln)
    return jax.nn.silu(z)


def memory_cross_attention(h, m, w_cq, w_ck, w_cv, w_co):
    B, S, _ = h.shape
    M = m.shape[1]
    q = (h @ w_cq).reshape(B, S, N_CROSS_HEADS, CROSS_HEAD_DIM)
    k = (m @ w_ck).reshape(B, M, N_CROSS_HEADS, CROSS_HEAD_DIM)
    v = (m @ w_cv).reshape(B, M, N_CROSS_HEADS, CROSS_HEAD_DIM)
    s = jnp.einsum('bqhd,bkhd->bhqk', q, k).astype(F32) * (CROSS_HEAD_DIM ** -0.5)
    p = jax.nn.softmax(s, axis=-1)
    o = jnp.einsum('bhqk,bkhd->bqhd', p.astype(v.dtype), v).reshape(B, S, CROSS_WIDTH)
    return o @ w_co


def clamped_swiglu(u):
    a, b = jnp.split(u, 2, axis=-1)
    a = jnp.minimum(a, SWIGLU_LIMIT)
    b = jnp.clip(b, -SWIGLU_LIMIT, SWIGLU_LIMIT)
    return a * jax.nn.sigmoid(SWIGLU_ALPHA * a) * (b + 1.0)


def moe_ffn(h, w_router, b_router, w1, b1, w2, b2):
    N, D = h.shape
    logits = (h @ w_router).astype(F32) + b_router.astype(F32)
    top_val, top_idx = lax.top_k(logits, TOP_K)
    gate = jax.nn.softmax(top_val, axis=-1)
    NK = N * TOP_K
    P = NK + N_EXPERTS * MOE_BLOCK
    n_blocks = P // MOE_BLOCK
    flat_e = top_idx.reshape(NK).astype(jnp.int32)
    order = jnp.argsort(flat_e).astype(jnp.int32)
    sorted_e = flat_e[order]
    counts = jnp.bincount(flat_e, length=N_EXPERTS).astype(jnp.int32)
    padded = (counts + MOE_BLOCK - 1) // MOE_BLOCK * MOE_BLOCK
    pad_end = jnp.cumsum(padded)
    pad_start = pad_end - padded
    start = jnp.cumsum(counts) - counts
    dest = pad_start[sorted_e] + jnp.arange(NK, dtype=jnp.int32) - start[sorted_e]
    row_tok = jnp.full((P,), N, jnp.int32).at[dest].set(order // TOP_K)
    row_gate = jnp.zeros((P,), F32).at[dest].set(gate.reshape(NK)[order])
    block_start = jnp.arange(n_blocks, dtype=jnp.int32) * MOE_BLOCK
    block_e = jnp.minimum(jnp.searchsorted(pad_end, block_start, side='right'),
                          N_EXPERTS - 1).astype(jnp.int32)
    h_pad = jnp.concatenate([h, jnp.zeros((1, D), h.dtype)], axis=0)

    def block_step(acc, blk):
        tok, g, e = blk
        xb = h_pad[tok]
        u = xb @ w1[e] + b1[e]
        y = clamped_swiglu(u) @ w2[e] + b2[e]
        return acc.at[tok].add(y * g[:, None].astype(y.dtype)), None

    acc, _ = lax.scan(block_step, jnp.zeros((N + 1, D), h.dtype),
                      (row_tok.reshape(n_blocks, MOE_BLOCK),
                       row_gate.reshape(n_blocks, MOE_BLOCK), block_e))
    return acc[:N]


def setup_inputs(seed: int = 0) -> dict:
    key = jax.random.key(seed)
    ks = jax.random.split(key, 32)
    L, D, H, DH = DEPTH, D_MODEL, N_DIFF_HEADS, DIFF_HEAD_DIM
    E, F = N_EXPERTS, D_FF_EXPERT

    def nrm(k, shape, scale):
        return jax.random.normal(k, shape, F32) * scale

    def gain(k, shape):
        return 1.0 + 0.05 * jax.random.normal(k, shape, F32)

    return {
        'x': nrm(ks[0], (BATCH, SEQ, D), 1.0),
        'mem': nrm(ks[1], (BATCH, N_MEM, D), 1.0),
        'g_mix': gain(ks[2], (L, D)),
        'w_in': nrm(ks[3], (L, D, IN_WIDTH), D ** -0.5),
        'lambda_q1': nrm(ks[4], (L, DH), 0.1),
        'lambda_k1': nrm(ks[5], (L, DH), 0.1),
        'lambda_q2': nrm(ks[6], (L, DH), 0.1),
        'lambda_k2': nrm(ks[7], (L, DH), 0.1),
        'g_subln': gain(ks[8], (L, 2 * DH)),
        'w_dw': nrm(ks[9], (L, CONV_KERNEL, CONV_WIDTH), CONV_KERNEL ** -0.5),
        'b_dw': nrm(ks[10], (L, CONV_WIDTH), 0.02),
        'g_conv_ln': gain(ks[11], (L, CONV_WIDTH)),
        'b_conv_ln': nrm(ks[12], (L, CONV_WIDTH), 0.02),
        'w_out': nrm(ks[13], (L, MIX_WIDTH, D), MIX_WIDTH ** -0.5),
        'rel_bias': nrm(ks[14], (N_BUCKETS, H), 0.2),
        'g_cross': gain(ks[15], (L, D)),
        'g_mem': gain(ks[16], (L, D)),
        'w_cq': nrm(ks[17], (L, D, CROSS_WIDTH), D ** -0.5),
        'w_ck': nrm(ks[18], (L, D, CROSS_WIDTH), D ** -0.5),
        'w_cv': nrm(ks[19], (L, D, CROSS_WIDTH), D ** -0.5),
        'w_co': nrm(ks[20], (L, CROSS_WIDTH, D), CROSS_WIDTH ** -0.5),
        'g_ffn': gain(ks[21], (L, D)),
        'w_router': nrm(ks[22], (L, D, E), D ** -0.5),
        'b_router': nrm(ks[23], (L, E), 0.01),
        'w1': nrm(ks[24], (L, E, D, 2 * F), D ** -0.5),
        'b1': nrm(ks[25], (L, E, 2 * F), 0.01),
        'w2': nrm(ks[26], (L, E, F, D), F ** -0.5),
        'b2': nrm(ks[27], (L, E, D), 0.01),
        'g_final': gain(ks[28], (D,)),
    }


def reference(x, mem, g_mix, w_in, lambda_q1, lambda_k1, lambda_q2, lambda_k2, g_subln,
              w_dw, b_dw, g_conv_ln, b_conv_ln, w_out, rel_bias, g_cross, g_mem,
              w_cq, w_ck, w_cv, w_co, g_ffn, w_router, b_router, w1, b1, w2, b2, g_final):
    B, S, D = x.shape
    for l in range(DEPTH):
        lambda_init = 0.8 - 0.6 * math.exp(-0.3 * l)
        h = rms_norm(x, g_mix[l])
        u = h @ w_in[l]
        attn = diff_attention_group(u[..., :3 * ATTN_WIDTH], lambda_q1[l], lambda_k1[l],
                                    lambda_q2[l], lambda_k2[l], g_subln[l], rel_bias,
                                    lambda_init)
        conv = conformer_conv_group(u[..., 3 * ATTN_WIDTH:], w_dw[l], b_dw[l],
                                    g_conv_ln[l], b_conv_ln[l])
        x = x + jnp.concatenate([attn, conv], axis=-1) @ w_out[l]
        x = x + memory_cross_attention(rms_norm(x, g_cross[l]), rms_norm(mem, g_mem[l]),
                                       w_cq[l], w_ck[l], w_cv[l], w_co[l])
        hf = rms_norm(x, g_ffn[l]).reshape(B * S, D)
        x = x + moe_ffn(hf, w_router[l], b_router[l], w1[l], b1[l], w2[l], b2[l]).reshape(B, S, D)
    return rms_norm(x, g_final)
```

```python
import functools
import math

import jax
import jax.numpy as jnp
from jax import lax
from jax.experimental import pallas as pl
from jax.experimental.pallas import tpu as pltpu

F32 = jnp.float32
BF16 = jnp.bfloat16
U32 = jnp.uint32

DIFF_HEAD_DIM = 128
CONV_KERNEL = 31
N_BUCKETS = 32
MAX_DISTANCE = 128
N_CROSS_HEADS = 4
TOP_K = 4
SWIGLU_LIMIT = 7.0
SWIGLU_ALPHA = 1.702
EPS = 1e-6

V7X_VMEM_BYTES = 64 * 1024 * 1024
VMEM_LIMIT_BYTES = V7X_VMEM_BYTES - 6 * 1024 * 1024
LANES = 128
HALO_ROWS = 16

_T5_LOG_THRESHOLDS = tuple(
    math.ceil(8 * (MAX_DISTANCE / 8) ** (k / 8) - 1e-9) for k in range(1, 8))


def _params(*semantics):
    return pltpu.CompilerParams(dimension_semantics=semantics,
                                vmem_limit_bytes=VMEM_LIMIT_BYTES)


def _rmsnorm_kernel(x_ref, g_ref, o_ref):
    x = x_ref[...]
    y = x * lax.rsqrt(jnp.mean(x * x, axis=-1, keepdims=True) + EPS)
    o_ref[...] = (y * g_ref[...]).astype(o_ref.dtype)


def _rmsnorm(x, g, *, rows):
    n, d = x.shape
    return pl.pallas_call(
        _rmsnorm_kernel,
        out_shape=jax.ShapeDtypeStruct((n, d), BF16),
        grid=(n // rows,),
        in_specs=[pl.BlockSpec((rows, d), lambda i: (i, 0)),
                  pl.BlockSpec((1, d), lambda i: (0, 0))],
        out_specs=pl.BlockSpec((rows, d), lambda i: (i, 0)),
        compiler_params=_params("arbitrary"),
        name="rmsnorm",
    )(x, g.reshape(1, d))


def _dense_kernel(*refs, k_splits, has_res):
    n_lhs = len(k_splits)
    x_refs = refs[:n_lhs]
    w_ref = refs[n_lhs]
    res_ref = refs[n_lhs + 1] if has_res else None
    o_ref, wbf_ref = refs[-2], refs[-1]

    @pl.when(pl.program_id(1) == 0)
    def _():
        wbf_ref[...] = w_ref[...].astype(BF16)

    acc = None
    k0 = 0
    for x_ref, kw in zip(x_refs, k_splits):
        part = jnp.dot(x_ref[...], wbf_ref[k0:k0 + kw, :], preferred_element_type=F32)
        acc = part if acc is None else acc + part
        k0 += kw
    if has_res:
        acc = res_ref[...] + acc
    o_ref[...] = acc.astype(o_ref.dtype)


def _dense(xs, w, layer, *, tm, tn, out_dtype, res=None, name):
    m = xs[0].shape[0]
    k_splits = tuple(x.shape[1] for x in xs)
    k, n = w.shape[1], w.shape[2]
    assert sum(k_splits) == k and m % tm == 0 and n % tn == 0
    in_specs = [pl.BlockSpec((tm, kw), lambda j, i: (i, 0)) for kw in k_splits]
    in_specs.append(pl.BlockSpec((None, k, tn), lambda j, i: (layer, 0, j)))
    args = list(xs) + [w]
    if res is not None:
        in_specs.append(pl.BlockSpec((tm, tn), lambda j, i: (i, j)))
        args.append(res)
    return pl.pallas_call(
        functools.partial(_dense_kernel, k_splits=k_splits, has_res=res is not None),
        out_shape=jax.ShapeDtypeStruct((m, n), out_dtype),
        grid=(n // tn, m // tm),
        in_specs=in_specs,
        out_specs=pl.BlockSpec((tm, tn), lambda j, i: (i, j)),
        scratch_shapes=[pltpu.VMEM((k, tn), BF16)],
        compiler_params=_params("arbitrary", "arbitrary"),
        name=name,
    )(*args)


def _t5_bucket(rel):
    half = N_BUCKETS // 2
    max_exact = half // 2
    n = jnp.abs(rel)
    large = jnp.full(rel.shape, max_exact, jnp.int32)
    for thr in _T5_LOG_THRESHOLDS:
        large = large + (n >= thr).astype(jnp.int32)
    return jnp.where(rel > 0, half, 0) + jnp.where(n < max_exact, n, large)


def _attn_prep_kernel(rb_ref, lq1_ref, lk1_ref, lq2_ref, lk2_ref, band_ref, lam_ref, *,
                      tq, lambda_init):
    h = pl.program_id(0)
    shape = (tq, tq + 2 * LANES)
    a = lax.broadcasted_iota(jnp.int32, shape, 0)
    d = lax.broadcasted_iota(jnp.int32, shape, 1)
    bucket = _t5_bucket(d - LANES - a)
    val = jnp.zeros(shape, F32)
    for b in range(N_BUCKETS):
        val = jnp.where(bucket == b, rb_ref[b, h], val)
    far = jnp.where(d < LANES + tq // 2, rb_ref[N_BUCKETS // 2 - 1, h], rb_ref[N_BUCKETS - 1, h])
    band_ref[0] = val - far
    s1 = jnp.sum(lq1_ref[...] * lk1_ref[...], axis=-1, keepdims=True)
    s2 = jnp.sum(lq2_ref[...] * lk2_ref[...], axis=-1, keepdims=True)
    lam_ref[...] = jnp.exp(s1) - jnp.exp(s2) + lambda_init


def _attn_prep(rel_bias, lq1, lk1, lq2, lk2, *, tq, lambda_init):
    n_heads = rel_bias.shape[1]
    vec = pl.BlockSpec((1, DIFF_HEAD_DIM), lambda h: (0, 0))
    return pl.pallas_call(
        functools.partial(_attn_prep_kernel, tq=tq, lambda_init=lambda_init),
        out_shape=(jax.ShapeDtypeStruct((n_heads, tq, tq + 2 * LANES), F32),
                   jax.ShapeDtypeStruct((1, 1), F32)),
        grid=(n_heads,),
        in_specs=[pl.BlockSpec(memory_space=pltpu.SMEM), vec, vec, vec, vec],
        out_specs=(pl.BlockSpec((1, tq, tq + 2 * LANES), lambda h: (h, 0, 0)),
                   pl.BlockSpec((1, 1), lambda h: (0, 0))),
        compiler_params=_params("arbitrary"),
        name="attn_prep",
    )(rel_bias, lq1.reshape(1, -1), lk1.reshape(1, -1), lq2.reshape(1, -1), lk2.reshape(1, -1))


def _diff_attn_kernel(lam_ref, rb_ref, q_ref, k_ref, v_ref, band_ref, g_ref, o_ref,
                      s1_ref, s2_ref, *, tq, seq, lambda_init):
    h = pl.program_id(1)
    i = pl.program_id(2)
    dh = DIFF_HEAD_DIM
    scale = dh ** -0.5
    q = q_ref[0]
    k = k_ref[0]
    kpos = lax.broadcasted_iota(jnp.int32, (1, seq), 1)
    far = jnp.where(kpos < i * tq + tq // 2,
                    rb_ref[N_BUCKETS // 2 - 1, h], rb_ref[N_BUCKETS - 1, h])
    col0 = pl.multiple_of(i * tq, LANES)
    pad = jnp.zeros((tq, LANES), F32)
    probs = []
    for m, s_ref in enumerate((s1_ref, s2_ref)):
        s = lax.dot_general(q[:, m * dh:(m + 1) * dh], k[:, m * dh:(m + 1) * dh],
                            (((1,), (1,)), ((), ())), preferred_element_type=F32)
        s_ref[:, :LANES] = pad
        s_ref[:, LANES + seq:] = pad
        s_ref[:, LANES:LANES + seq] = s * scale + far
        s_ref[:, pl.ds(col0, tq + 2 * LANES)] += band_ref[0]
        sv = s_ref[:, LANES:LANES + seq]
        e = jnp.exp(sv - jnp.max(sv, axis=-1, keepdims=True))
        probs.append((e, jnp.sum(e, axis=-1, keepdims=True)))
    (e1, l1), (e2, l2) = probs
    p = e1 * (1.0 / l1) - e2 * (lam_ref[0, 0] / l2)
    o = jnp.dot(p.astype(BF16), v_ref[0], preferred_element_type=F32)
    y = o * lax.rsqrt(jnp.mean(o * o, axis=-1, keepdims=True) + EPS)
    o_ref[0] = ((y * g_ref[...]) * (1.0 - lambda_init)).astype(o_ref.dtype)


def _diff_attention(u, band, lam, rel_bias, g_subln, *, batch, seq, n_heads, tq, lambda_init):
    hw = 2 * DIFF_HEAD_DIM
    return pl.pallas_call(
        functools.partial(_diff_attn_kernel, tq=tq, seq=seq, lambda_init=lambda_init),
        out_shape=jax.ShapeDtypeStruct((batch, seq, n_heads * hw), BF16),
        grid=(batch, n_heads, seq // tq),
        in_specs=[
            pl.BlockSpec(memory_space=pltpu.SMEM),
            pl.BlockSpec(memory_space=pltpu.SMEM),
            pl.BlockSpec((1, tq, hw), lambda b, h, i: (b, i, h)),
            pl.BlockSpec((1, seq, hw), lambda b, h, i: (b, 0, n_heads + h)),
            pl.BlockSpec((1, seq, hw), lambda b, h, i: (b, 0, 2 * n_heads + h)),
            pl.BlockSpec((1, tq, tq + 2 * LANES), lambda b, h, i: (h, 0, 0)),
            pl.BlockSpec((1, hw), lambda b, h, i: (0, 0)),
        ],
        out_specs=pl.BlockSpec((1, tq, hw), lambda b, h, i: (b, i, h)),
        scratch_shapes=[pltpu.VMEM((tq, seq + 2 * LANES), F32),
                        pltpu.VMEM((tq, seq + 2 * LANES), F32)],
        compiler_params=_params("arbitrary", "arbitrary", "arbitrary"),
        name="diff_attention",
    )(lam, rel_bias, u, u, u, band, g_subln.reshape(1, hw))


def _conv_kernel(a_ref, ap_ref, an_ref, g_ref, gp_ref, gn_ref, w_ref, b_ref, lg_ref, lb_ref,
                 o_ref, z_ref, *, ts):
    i = pl.program_id(1)
    last = pl.num_programs(1) - 1

    def glu(a, g):
        return a[0].astype(F32) * jax.nn.sigmoid(g[0].astype(F32))

    z_ref[:HALO_ROWS, :] = glu(ap_ref, gp_ref) * (i > 0).astype(F32)
    z_ref[HALO_ROWS:HALO_ROWS + ts, :] = glu(a_ref, g_ref)
    z_ref[HALO_ROWS + ts:, :] = glu(an_ref, gn_ref) * (i < last).astype(F32)
    first = HALO_ROWS - CONV_KERNEL // 2
    acc = jnp.zeros(o_ref.shape[1:], F32)
    for t in range(CONV_KERNEL):
        acc = acc + w_ref[t:t + 1, :] * z_ref[first + t:first + t + ts, :]
    y = acc + b_ref[...]
    yc = y - jnp.mean(y, axis=-1, keepdims=True)
    yn = yc * lax.rsqrt(jnp.mean(yc * yc, axis=-1, keepdims=True) + EPS)
    yn = yn * lg_ref[...] + lb_ref[...]
    o_ref[0] = (yn * jax.nn.sigmoid(yn)).astype(o_ref.dtype)


def _conv_module(u, w_dw, b_dw, g_ln, b_ln, *, batch, seq, col_block, ts):
    cw = w_dw.shape[1]
    hb = ts // HALO_ROWS
    n_hblocks = seq // HALO_ROWS

    def main(c):
        return pl.BlockSpec((1, ts, cw), lambda b, i: (b, i, c))

    def prev(c):
        return pl.BlockSpec((1, HALO_ROWS, cw), lambda b, i: (b, jnp.maximum(i * hb - 1, 0), c))

    def nxt(c):
        return pl.BlockSpec((1, HALO_ROWS, cw),
                            lambda b, i: (b, jnp.minimum((i + 1) * hb, n_hblocks - 1), c))

    row = pl.BlockSpec((1, cw), lambda b, i: (0, 0))
    return pl.pallas_call(
        functools.partial(_conv_kernel, ts=ts),
        out_shape=jax.ShapeDtypeStruct((batch, seq, cw), BF16),
        grid=(batch, seq // ts),
        in_specs=[main(col_block), prev(col_block), nxt(col_block),
                  main(col_block + 1), prev(col_block + 1), nxt(col_block + 1),
                  pl.BlockSpec((CONV_KERNEL, cw), lambda b, i: (0, 0)), row, row, row],
        out_specs=pl.BlockSpec((1, ts, cw), lambda b, i: (b, i, 0)),
        scratch_shapes=[pltpu.VMEM((ts + 2 * HALO_ROWS, cw), F32)],
        compiler_params=_params("arbitrary", "arbitrary"),
        name="conv_module",
    )(u, u, u, u, u, u, w_dw, b_dw.reshape(1, cw), g_ln.reshape(1, cw), b_ln.reshape(1, cw))


def _cross_attn_kernel(q_ref, k_ref, v_ref, o_ref, *, head_dim):
    scale = head_dim ** -0.5
    for h in range(N_CROSS_HEADS):
        cols = slice(h * head_dim, (h + 1) * head_dim)
        s = lax.dot_general(q_ref[:, cols], k_ref[:, cols], (((1,), (1,)), ((), ())),
                            preferred_element_type=F32) * scale
        e = jnp.exp(s - jnp.max(s, axis=-1, keepdims=True))
        p = e * (1.0 / jnp.sum(e, axis=-1, keepdims=True))
        o_ref[:, cols] = jnp.dot(p.astype(BF16), v_ref[:, cols],
                                 preferred_element_type=F32).astype(o_ref.dtype)


def _cross_attention(q, k, v, *, batch, seq, n_mem, tm):
    width = q.shape[1]
    qb = seq // tm
    return pl.pallas_call(
        functools.partial(_cross_attn_kernel, head_dim=width // N_CROSS_HEADS),
        out_shape=jax.ShapeDtypeStruct(q.shape, BF16),
        grid=(batch, qb),
        in_specs=[pl.BlockSpec((tm, width), lambda b, i: (b * qb + i, 0)),
                  pl.BlockSpec((n_mem, width), lambda b, i: (b, 0)),
                  pl.BlockSpec((n_mem, width), lambda b, i: (b, 0))],
        out_specs=pl.BlockSpec((tm, width), lambda b, i: (b * qb + i, 0)),
        compiler_params=_params("arbitrary", "arbitrary"),
        name="cross_attention",
    )(q, k, v)


def _pack_bf16_pairs(lo_half, hi_half):
    a = pltpu.bitcast(lo_half, U32)
    b = pltpu.bitcast(hi_half, U32)
    return (a & jnp.uint32(0xFFFF0000)) | (b >> 16)


def _unpack_bf16_pairs(words):
    a = pltpu.bitcast(words & jnp.uint32(0xFFFF0000), F32)
    b = pltpu.bitcast(words << 16, F32)
    return a.astype(BF16), b.astype(BF16)


def _router_kernel(x_ref, g_ref, w_ref, b_ref, hp_ref, idx_ref, gate_ref):
    x = x_ref[...]
    d = x.shape[1]
    y = x * lax.rsqrt(jnp.mean(x * x, axis=-1, keepdims=True) + EPS)
    hb = (y * g_ref[...]).astype(BF16)
    hp_ref[...] = _pack_bf16_pairs(hb[:, :d // 2].astype(F32), hb[:, d // 2:].astype(F32))
    logits = jnp.dot(hb, w_ref[...].astype(BF16), preferred_element_type=F32) + b_ref[...]
    n_exp = logits.shape[1]
    lane = lax.broadcasted_iota(jnp.int32, logits.shape, 1)
    kl = lax.broadcasted_iota(jnp.int32, idx_ref.shape, 1)
    vals = logits
    top_v = jnp.zeros(gate_ref.shape, F32)
    top_i = jnp.zeros(idx_ref.shape, jnp.int32)
    for kk in range(TOP_K):
        mx = jnp.max(vals, axis=-1, keepdims=True)
        sel = jnp.min(jnp.where(vals == mx, lane, n_exp), axis=-1, keepdims=True)
        top_v = jnp.where(kl == kk, mx, top_v)
        top_i = jnp.where(kl == kk, sel, top_i)
        vals = jnp.where(lane == sel, -jnp.inf, vals)
    e = jnp.exp(top_v - jnp.max(top_v, axis=-1, keepdims=True))
    gate_ref[...] = e / jnp.sum(e, axis=-1, keepdims=True)
    idx_ref[...] = top_i


def _router(x, g, w_router, b_router, layer, *, tm):
    n, d = x.shape
    n_exp = w_router.shape[2]
    return pl.pallas_call(
        _router_kernel,
        out_shape=(jax.ShapeDtypeStruct((n, d // 2), U32),
                   jax.ShapeDtypeStruct((n, TOP_K), jnp.int32),
                   jax.ShapeDtypeStruct((n, TOP_K), F32)),
        grid=(n // tm,),
        in_specs=[pl.BlockSpec((tm, d), lambda i: (i, 0)),
                  pl.BlockSpec((1, d), lambda i: (0, 0)),
                  pl.BlockSpec((None, d, n_exp), lambda i: (layer, 0, 0)),
                  pl.BlockSpec((1, n_exp), lambda i: (0, 0))],
        out_specs=(pl.BlockSpec((tm, d // 2), lambda i: (i, 0)),
                   pl.BlockSpec((tm, TOP_K), lambda i: (i, 0)),
                   pl.BlockSpec((tm, TOP_K), lambda i: (i, 0))),
        compiler_params=_params("arbitrary"),
        name="router",
    )(x, g.reshape(1, d), w_router, b_router.reshape(1, n_exp))


def _dispatch_kernel(tok_ref, src_ref, dst_ref, sem, *, rows):
    t = pl.program_id(0)

    def issue(r, carry):
        tok = tok_ref[0, 0, r]
        pltpu.make_async_copy(src_ref.at[pl.ds(tok, 1)],
                              dst_ref.at[pl.ds(t * rows + r, 1)], sem.at[0]).start()
        return carry

    def drain(r, carry):
        pltpu.make_async_copy(src_ref.at[pl.ds(0, 1)], dst_ref.at[pl.ds(r, 1)], sem.at[0]).wait()
        return carry

    lax.fori_loop(0, rows, issue, 0)

    @pl.when(t > 0)
    def _():
        lax.fori_loop(0, rows, drain, 0)

    @pl.when(t == pl.num_programs(0) - 1)
    def _():
        lax.fori_loop(0, rows, drain, 0)


def _dispatch(src, row_tok, *, rows):
    p = row_tok.shape[0]
    return pl.pallas_call(
        functools.partial(_dispatch_kernel, rows=rows),
        out_shape=jax.ShapeDtypeStruct((p, src.shape[1]), src.dtype),
        grid=(p // rows,),
        in_specs=[pl.BlockSpec((1, 1, rows), lambda t: (t, 0, 0), memory_space=pltpu.SMEM),
                  pl.BlockSpec(memory_space=pl.ANY)],
        out_specs=pl.BlockSpec(memory_space=pl.ANY),
        scratch_shapes=[pltpu.SemaphoreType.DMA((1,))],
        compiler_params=_params("arbitrary"),
        name="moe_dispatch",
    )(row_tok.reshape(p // rows, 1, rows), src)


def _is_first_tile_of_expert(te_ref, t):
    return jnp.logical_or(t == 0, te_ref[t] != te_ref[jnp.maximum(t - 1, 0)])


def _gmm1_kernel(te_ref, nu_ref, x_ref, wa_ref, wb_ref, ba_ref, bb_ref, o_ref, wa_bf, wb_bf):
    t = pl.program_id(1)

    @pl.when(t < nu_ref[0])
    def _():
        @pl.when(_is_first_tile_of_expert(te_ref, t))
        def _():
            wa_bf[...] = wa_ref[...].astype(BF16)
            wb_bf[...] = wb_ref[...].astype(BF16)

        x_lo, x_hi = _unpack_bf16_pairs(x_ref[...])
        half = x_lo.shape[1]

        def proj(w_bf, b_ref):
            return (jnp.dot(x_lo, w_bf[:half, :], preferred_element_type=F32)
                    + jnp.dot(x_hi, w_bf[half:, :], preferred_element_type=F32) + b_ref[...])

        a = jnp.minimum(proj(wa_bf, ba_ref), SWIGLU_LIMIT)
        b = jnp.clip(proj(wb_bf, bb_ref), -SWIGLU_LIMIT, SWIGLU_LIMIT)
        o_ref[...] = (a * jax.nn.sigmoid(SWIGLU_ALPHA * a) * (b + 1.0)).astype(o_ref.dtype)


def _gmm1(xg, w1, b1, layer, tile_e, n_used, *, tm, tf):
    p, half = xg.shape
    d = 2 * half
    n_exp, ff = w1.shape[1], w1.shape[3] // 2
    nj = ff // tf
    assert ff % tf == 0 and p % tm == 0

    def row(t, nu):
        return jnp.minimum(t, nu[0] - 1)

    return pl.pallas_call(
        _gmm1_kernel,
        out_shape=jax.ShapeDtypeStruct((p, ff), BF16),
        grid_spec=pltpu.PrefetchScalarGridSpec(
            num_scalar_prefetch=2,
            grid=(nj, p // tm),
            in_specs=[
                pl.BlockSpec((tm, half), lambda j, t, te, nu: (row(t, nu), 0)),
                pl.BlockSpec((None, None, d, tf), lambda j, t, te, nu: (layer, te[t], 0, j)),
                pl.BlockSpec((None, None, d, tf), lambda j, t, te, nu: (layer, te[t], 0, nj + j)),
                pl.BlockSpec((None, None, 1, tf), lambda j, t, te, nu: (layer, te[t], 0, j)),
                pl.BlockSpec((None, None, 1, tf), lambda j, t, te, nu: (layer, te[t], 0, nj + j)),
            ],
            out_specs=pl.BlockSpec((tm, tf), lambda j, t, te, nu: (row(t, nu), j)),
            scratch_shapes=[pltpu.VMEM((d, tf), BF16), pltpu.VMEM((d, tf), BF16)],
        ),
        compiler_params=_params("arbitrary", "arbitrary"),
        name="moe_gmm1",
    )(tile_e, n_used, xg, w1, w1, b1.reshape(b1.shape[0], n_exp, 1, 2 * ff),
      b1.reshape(b1.shape[0], n_exp, 1, 2 * ff))


def _gmm2_kernel(te_ref, nu_ref, x_ref, w_ref, b_ref, o_ref, w_bf):
    t = pl.program_id(1)

    @pl.when(t < nu_ref[0])
    def _():
        @pl.when(_is_first_tile_of_expert(te_ref, t))
        def _():
            w_bf[...] = w_ref[...].astype(BF16)

        o_ref[...] = jnp.dot(x_ref[...], w_bf[...], preferred_element_type=F32) + b_ref[...]


def _gmm2(act, w2, b2, layer, tile_e, n_used, *, tm, tn):
    p, ff = act.shape
    n_exp, d = w2.shape[1], w2.shape[3]
    assert d % tn == 0

    def row(t, nu):
        return jnp.minimum(t, nu[0] - 1)

    return pl.pallas_call(
        _gmm2_kernel,
        out_shape=jax.ShapeDtypeStruct((p, d), F32),
        grid_spec=pltpu.PrefetchScalarGridSpec(
            num_scalar_prefetch=2,
            grid=(d // tn, p // tm),
            in_specs=[
                pl.BlockSpec((tm, ff), lambda j, t, te, nu: (row(t, nu), 0)),
                pl.BlockSpec((None, None, ff, tn), lambda j, t, te, nu: (layer, te[t], 0, j)),
                pl.BlockSpec((None, None, 1, tn), lambda j, t, te, nu: (layer, te[t], 0, j)),
            ],
            out_specs=pl.BlockSpec((tm, tn), lambda j, t, te, nu: (row(t, nu), j)),
            scratch_shapes=[pltpu.VMEM((ff, tn), BF16)],
        ),
        compiler_params=_params("arbitrary", "arbitrary"),
        name="moe_gmm2",
    )(tile_e, n_used, act, w2, b2.reshape(b2.shape[0], n_exp, 1, d))


def _combine_kernel(dcur_ref, dnext_ref, x_ref, gate_ref, g_ref, y_ref, o_ref, buf, sem, *,
                    tt, final_norm):
    t = pl.program_id(0)
    slot = lax.rem(t, 2)

    def row_copy(d_ref, r, kk, s):
        return pltpu.make_async_copy(y_ref.at[pl.ds(d_ref[0, 0, r * TOP_K + kk], 1)],
                                     buf.at[s, kk, pl.ds(r, 1)], sem.at[s])

    def issue(d_ref, s):
        def body(r, carry):
            for kk in range(TOP_K):
                row_copy(d_ref, r, kk, s).start()
            return carry
        lax.fori_loop(0, tt, body, 0)

    @pl.when(t == 0)
    def _():
        issue(dcur_ref, 0)

    @pl.when(t + 1 < pl.num_programs(0))
    def _():
        issue(dnext_ref, 1 - slot)

    def drain(r, carry):
        for kk in range(TOP_K):
            row_copy(dcur_ref, r, kk, slot).wait()
        return carry
    lax.fori_loop(0, tt, drain, 0)

    gate = gate_ref[...]
    acc = x_ref[...]
    for kk in range(TOP_K):
        acc = acc + gate[:, kk:kk + 1] * buf[slot, kk]
    if final_norm:
        acc = acc * lax.rsqrt(jnp.mean(acc * acc, axis=-1, keepdims=True) + EPS) * g_ref[...]
    o_ref[...] = acc


def _combine(x, y, dest, gate, g_final, *, tt, final_norm):
    n, d = x.shape
    nt = n // tt
    dest3 = dest.reshape(nt, 1, tt * TOP_K)
    return pl.pallas_call(
        functools.partial(_combine_kernel, tt=tt, final_norm=final_norm),
        out_shape=jax.ShapeDtypeStruct((n, d), F32),
        grid=(nt,),
        in_specs=[
            pl.BlockSpec((1, 1, tt * TOP_K), lambda t: (t, 0, 0), memory_space=pltpu.SMEM),
            pl.BlockSpec((1, 1, tt * TOP_K), lambda t: (jnp.minimum(t + 1, nt - 1), 0, 0),
                         memory_space=pltpu.SMEM),
            pl.BlockSpec((tt, d), lambda t: (t, 0)),
            pl.BlockSpec((tt, TOP_K), lambda t: (t, 0)),
            pl.BlockSpec((1, d), lambda t: (0, 0)),
            pl.BlockSpec(memory_space=pl.ANY),
        ],
        out_specs=pl.BlockSpec((tt, d), lambda t: (t, 0)),
        scratch_shapes=[pltpu.VMEM((2, TOP_K, tt, d), F32), pltpu.SemaphoreType.DMA((2,))],
        compiler_params=_params("arbitrary"),
        name="moe_combine",
    )(dest3, dest3, x, gate, g_final.reshape(1, d), y)


def _routing_tables(top_idx, n_exp, tm):
    n = top_idx.shape[0]
    nk = n * TOP_K
    p = nk + n_exp * tm
    flat_e = top_idx.reshape(nk)
    order = jnp.argsort(flat_e).astype(jnp.int32)
    sorted_e = flat_e[order]
    counts = jnp.bincount(flat_e, length=n_exp).astype(jnp.int32)
    padded = (counts + tm - 1) // tm * tm
    pad_end = jnp.cumsum(padded)
    pad_start = pad_end - padded
    start = jnp.cumsum(counts) - counts
    dest_sorted = pad_start[sorted_e] + jnp.arange(nk, dtype=jnp.int32) - start[sorted_e]
    row_tok = jnp.zeros((p,), jnp.int32).at[dest_sorted].set(order // TOP_K)
    dest = jnp.zeros((nk,), jnp.int32).at[order].set(dest_sorted)
    tile_start = jnp.arange(p // tm, dtype=jnp.int32) * tm
    tile_e = jnp.minimum(jnp.searchsorted(pad_end, tile_start, side="right"),
                         n_exp - 1).astype(jnp.int32)
    n_used = (pad_end[-1:] // tm).astype(jnp.int32)
    return row_tok, dest, tile_e, n_used


def _pick(n, pref):
    t = min(n, pref)
    while n % t or t % 8:
        t -= 1
    return t


def kernel(x, mem, g_mix, w_in, lambda_q1, lambda_k1, lambda_q2, lambda_k2, g_subln, w_dw, b_dw,
           g_conv_ln, b_conv_ln, w_out, rel_bias, g_cross, g_mem, w_cq, w_ck, w_cv, w_co, g_ffn,
           w_router, b_router, w1, b1, w2, b2, g_final):
    batch, seq, d = x.shape
    n = batch * seq
    n_mem = mem.shape[1]
    depth = g_mix.shape[0]
    n_heads = rel_bias.shape[1]
    attn_w = n_heads * 2 * DIFF_HEAD_DIM
    conv_w = w_dw.shape[2]
    in_w = w_in.shape[2]
    n_exp = w_router.shape[2]
    assert in_w == 3 * attn_w + 2 * conv_w and (3 * attn_w) % conv_w == 0

    tq = _pick(seq, 256)
    xf = x.reshape(n, d)
    memf = mem.reshape(batch * n_mem, d)
    for l in range(depth):
        lambda_init = 0.8 - 0.6 * math.exp(-0.3 * l)
        band, lam = _attn_prep(rel_bias, lambda_q1[l], lambda_k1[l], lambda_q2[l], lambda_k2[l],
                               tq=tq, lambda_init=lambda_init)
        h = _rmsnorm(xf, g_mix[l], rows=_pick(n, 256))
        u = _dense([h], w_in, l, tm=_pick(n, 512), tn=_pick(in_w, 1024), out_dtype=BF16,
                   name="in_proj").reshape(batch, seq, in_w)
        attn = _diff_attention(u, band, lam, rel_bias, g_subln[l], batch=batch, seq=seq,
                               n_heads=n_heads, tq=tq, lambda_init=lambda_init)
        conv = _conv_module(u, w_dw[l], b_dw[l], g_conv_ln[l], b_conv_ln[l], batch=batch, seq=seq,
                            col_block=3 * attn_w // conv_w, ts=_pick(seq, 256))
        xf = _dense([attn.reshape(n, attn_w), conv.reshape(n, conv_w)], w_out, l,
                    tm=_pick(n, 512), tn=_pick(d, 512), out_dtype=F32, res=xf, name="out_proj")
        h = _rmsnorm(xf, g_cross[l], rows=_pick(n, 256))
        m = _rmsnorm(memf, g_mem[l], rows=_pick(batch * n_mem, 256))
        cross_w = w_cq.shape[2]
        qc = _dense([h], w_cq, l, tm=_pick(n, 512), tn=_pick(cross_w, 512), out_dtype=BF16,
                    name="cross_q")
        kc = _dense([m], w_ck, l, tm=_pick(batch * n_mem, 512), tn=_pick(cross_w, 512),
                    out_dtype=BF16, name="cross_k")
        vc = _dense([m], w_cv, l, tm=_pick(batch * n_mem, 512), tn=_pick(cross_w, 512),
                    out_dtype=BF16, name="cross_v")
        oc = _cross_attention(qc, kc, vc, batch=batch, seq=seq, n_mem=n_mem, tm=_pick(seq, 512))
        xf = _dense([oc], w_co, l, tm=_pick(n, 512), tn=_pick(d, 1024), out_dtype=F32, res=xf,
                    name="cross_o")
        tm = 256
        hp, top_idx, gate = _router(xf, g_ffn[l], w_router, b_router[l], l, tm=_pick(n, 256))
        row_tok, dest, tile_e, n_used = _routing_tables(top_idx, n_exp, tm)
        xg = _dispatch(hp, row_tok, rows=tm)
        ff = w2.shape[2]
        act = _gmm1(xg, w1, b1, l, tile_e, n_used, tm=tm, tf=_pick(ff, 512))
        y = _gmm2(act, w2, b2, l, tile_e, n_used, tm=tm, tn=_pick(d, 2048))
        xf = _combine(xf, y, dest, gate, g_final, tt=_pick(n, 64), final_norm=l == depth - 1)
    return xf.reshape(batch, seq, d)
```

```python
import functools
import math

import jax
import jax.numpy as jnp
from jax import lax
from jax.experimental import pallas as pl
from jax.experimental.pallas import tpu as pltpu

F32 = jnp.float32
BF16 = jnp.bfloat16
U32 = jnp.uint32

DIFF_HEAD_DIM = 128
CONV_KERNEL = 31
N_BUCKETS = 32
MAX_DISTANCE = 128
N_CROSS_HEADS = 4
TOP_K = 4
SWIGLU_LIMIT = 7.0
SWIGLU_ALPHA = 1.702
EPS = 1e-6

V7X_VMEM_BYTES = 64 * 1024 * 1024
VMEM_LIMIT_BYTES = V7X_VMEM_BYTES - 6 * 1024 * 1024
LANES = 128
HALO_ROWS = 16

_T5_LOG_THRESHOLDS = tuple(
    math.ceil(8 * (MAX_DISTANCE / 8) ** (k / 8) - 1e-9) for k in range(1, 8))


def _params(*semantics):
    return pltpu.CompilerParams(dimension_semantics=semantics,
                                vmem_limit_bytes=VMEM_LIMIT_BYTES)


def _rmsnorm_kernel(x_ref, g_ref, o_ref):
    x = x_ref[...]
    y = x * lax.rsqrt(jnp.mean(x * x, axis=-1, keepdims=True) + EPS)
    o_ref[...] = (y * g_ref[...]).astype(o_ref.dtype)


def _rmsnorm(x, g, *, rows):
    n, d = x.shape
    return pl.pallas_call(
        _rmsnorm_kernel,
        out_shape=jax.ShapeDtypeStruct((n, d), BF16),
        grid=(n // rows,),
        in_specs=[pl.BlockSpec((rows, d), lambda i: (i, 0)),
                  pl.BlockSpec((1, d), lambda i: (0, 0))],
        out_specs=pl.BlockSpec((rows, d), lambda i: (i, 0)),
        compiler_params=_params("arbitrary"),
        name="rmsnorm",
    )(x, g.reshape(1, d))


def _dense_kernel(*refs, k_splits, has_res):
    n_lhs = len(k_splits)
    x_refs = refs[:n_lhs]
    w_ref = refs[n_lhs]
    res_ref = refs[n_lhs + 1] if has_res else None
    o_ref, wbf_ref = refs[-2], refs[-1]

    @pl.when(pl.program_id(1) == 0)
    def _():
        wbf_ref[...] = w_ref[...].astype(BF16)

    acc = None
    k0 = 0
    for x_ref, kw in zip(x_refs, k_splits):
        part = jnp.dot(x_ref[...], wbf_ref[k0:k0 + kw, :], preferred_element_type=F32)
        acc = part if acc is None else acc + part
        k0 += kw
    if has_res:
        acc = res_ref[...] + acc
    o_ref[...] = acc.astype(o_ref.dtype)


def _dense(xs, w, layer, *, tm, tn, out_dtype, res=None, name):
    m = xs[0].shape[0]
    k_splits = tuple(x.shape[1] for x in xs)
    k, n = w.shape[1], w.shape[2]
    assert sum(k_splits) == k and m % tm == 0 and n % tn == 0
    in_specs = [pl.BlockSpec((tm, kw), lambda j, i: (i, 0)) for kw in k_splits]
    in_specs.append(pl.BlockSpec((None, k, tn), lambda j, i: (layer, 0, j)))
    args = list(xs) + [w]
    if res is not None:
        in_specs.append(pl.BlockSpec((tm, tn), lambda j, i: (i, j)))
        args.append(res)
    return pl.pallas_call(
        functools.partial(_dense_kernel, k_splits=k_splits, has_res=res is not None),
        out_shape=jax.ShapeDtypeStruct((m, n), out_dtype),
        grid=(n // tn, m // tm),
        in_specs=in_specs,
        out_specs=pl.BlockSpec((tm, tn), lambda j, i: (i, j)),
        scratch_shapes=[pltpu.VMEM((k, tn), BF16)],
        compiler_params=_params("arbitrary", "arbitrary"),
        name=name,
    )(*args)


def _t5_bucket(rel):
    half = N_BUCKETS // 2
    max_exact = half // 2
    n = jnp.abs(rel)
    large = jnp.full(rel.shape, max_exact, jnp.int32)
    for thr in _T5_LOG_THRESHOLDS:
        large = large + (n >= thr).astype(jnp.int32)
    return jnp.where(rel > 0, half, 0) + jnp.where(n < max_exact, n, large)


def _attn_prep_kernel(rb_ref, lq1_ref, lk1_ref, lq2_ref, lk2_ref, band_ref, lam_ref, *,
                      tq, lambda_init):
    h = pl.program_id(0)
    shape = (tq, tq + 2 * LANES)
    a = lax.broadcasted_iota(jnp.int32, shape, 0)
    d = lax.broadcasted_iota(jnp.int32, shape, 1)
    bucket = _t5_bucket(d - LANES - a)
    val = jnp.zeros(shape, F32)
    for b in range(N_BUCKETS):
        val = jnp.where(bucket == b, rb_ref[b, h], val)
    far = jnp.where(d < LANES + tq // 2, rb_ref[N_BUCKETS // 2 - 1, h], rb_ref[N_BUCKETS - 1, h])
    band_ref[0] = val - far
    s1 = jnp.sum(lq1_ref[...] * lk1_ref[...], axis=-1, keepdims=True)
    s2 = jnp.sum(lq2_ref[...] * lk2_ref[...], axis=-1, keepdims=True)
    lam_ref[...] = jnp.exp(s1) - jnp.exp(s2) + lambda_init


def _attn_prep(rel_bias, lq1, lk1, lq2, lk2, *, tq, lambda_init):
    n_heads = rel_bias.shape[1]
    vec = pl.BlockSpec((1, DIFF_HEAD_DIM), lambda h: (0, 0))
    return pl.pallas_call(
        functools.partial(_attn_prep_kernel, tq=tq, lambda_init=lambda_init),
        out_shape=(jax.ShapeDtypeStruct((n_heads, tq, tq + 2 * LANES), F32),
                   jax.ShapeDtypeStruct((1, 1), F32)),
        grid=(n_heads,),
        in_specs=[pl.BlockSpec(memory_space=pltpu.SMEM), vec, vec, vec, vec],
        out_specs=(pl.BlockSpec((1, tq, tq + 2 * LANES), lambda h: (h, 0, 0)),
                   pl.BlockSpec((1, 1), lambda h: (0, 0))),
        compiler_params=_params("arbitrary"),
        name="attn_prep",
    )(rel_bias, lq1.reshape(1, -1), lk1.reshape(1, -1), lq2.reshape(1, -1), lk2.reshape(1, -1))


def _diff_attn_kernel(lam_ref, rb_ref, q_ref, k_ref, v_ref, band_ref, g_ref, o_ref,
                      s1_ref, s2_ref, *, tq, seq, lambda_init):
    h = pl.program_id(1)
    i = pl.program_id(2)
    dh = DIFF_HEAD_DIM
    scale = dh ** -0.5
    q = q_ref[0]
    k = k_ref[0]
    kpos = lax.broadcasted_iota(jnp.int32, (1, seq), 1)
    far = jnp.where(kpos < i * tq + tq // 2,
                    rb_ref[N_BUCKETS // 2 - 1, h], rb_ref[N_BUCKETS - 1, h])
    col0 = pl.multiple_of(i * tq, LANES)
    pad = jnp.zeros((tq, LANES), F32)
    probs = []
    for m, s_ref in enumerate((s1_ref, s2_ref)):
        s = lax.dot_general(q[:, m * dh:(m + 1) * dh], k[:, m * dh:(m + 1) * dh],
                            (((1,), (1,)), ((), ())), preferred_element_type=F32)
        s_ref[:, :LANES] = pad
        s_ref[:, LANES + seq:] = pad
        s_ref[:, LANES:LANES + seq] = s * scale + far
        s_ref[:, pl.ds(col0, tq + 2 * LANES)] += band_ref[0]
        sv = s_ref[:, LANES:LANES + seq]
        e = jnp.exp(sv - jnp.max(sv, axis=-1, keepdims=True))
        probs.append((e, jnp.sum(e, axis=-1, keepdims=True)))
    (e1, l1), (e2, l2) = probs
    p = e1 * (1.0 / l1) - e2 * (lam_ref[0, 0] / l2)
    o = jnp.dot(p.astype(BF16), v_ref[0], preferred_element_type=F32)
    y = o * lax.rsqrt(jnp.mean(o * o, axis=-1, keepdims=True) + EPS)
    o_ref[0] = ((y * g_ref[...]) * (1.0 - lambda_init)).astype(o_ref.dtype)


def _diff_attention(u, band, lam, rel_bias, g_subln, *, batch, seq, n_heads, tq, lambda_init):
    hw = 2 * DIFF_HEAD_DIM
    return pl.pallas_call(
        functools.partial(_diff_attn_kernel, tq=tq, seq=seq, lambda_init=lambda_init),
        out_shape=jax.ShapeDtypeStruct((batch, seq, n_heads * hw), BF16),
        grid=(batch, n_heads, seq // tq),
        in_specs=[
            pl.BlockSpec(memory_space=pltpu.SMEM),
            pl.BlockSpec(memory_space=pltpu.SMEM),
            pl.BlockSpec((1, tq, hw), lambda b, h, i: (b, i, h)),
            pl.BlockSpec((1, seq, hw), lambda b, h, i: (b, 0, n_heads + h)),
            pl.BlockSpec((1, seq, hw), lambda b, h, i: (b, 0, 2 * n_heads + h)),
            pl.BlockSpec((1, tq, tq + 2 * LANES), lambda b, h, i: (h, 0, 0)),
            pl.BlockSpec((1, hw), lambda b, h, i: (0, 0)),
        ],
        out_specs=pl.BlockSpec((1, tq, hw), lambda b, h, i: (b, i, h)),
        scratch_shapes=[pltpu.VMEM((tq, seq + 2 * LANES), F32),
                        pltpu.VMEM((tq, seq + 2 * LANES), F32)],
        compiler_params=_params("arbitrary", "arbitrary", "arbitrary"),
        name="diff_attention",
    )(lam, rel_bias, u, u, u, band, g_subln.reshape(1, hw))


def _conv_kernel(a_ref, ap_ref, an_ref, g_ref, gp_ref, gn_ref, w_ref, b_ref, lg_ref, lb_ref,
                 o_ref, z_ref, *, ts):
    i = pl.program_id(1)
    last = pl.num_programs(1) - 1

    def glu(a, g):
        return a[0].astype(F32) * jax.nn.sigmoid(g[0].astype(F32))

    z_ref[:HALO_ROWS, :] = glu(ap_ref, gp_ref) * (i > 0).astype(F32)
    z_ref[HALO_ROWS:HALO_ROWS + ts, :] = glu(a_ref, g_ref)
    z_ref[HALO_ROWS + ts:, :] = glu(an_ref, gn_ref) * (i < last).astype(F32)
    first = HALO_ROWS - CONV_KERNEL // 2
    acc = jnp.zeros(o_ref.shape[1:], F32)
    for t in range(CONV_KERNEL):
        acc = acc + w_ref[t:t + 1, :] * z_ref[first + t:first + t + ts, :]
    y = acc + b_ref[...]
    yc = y - jnp.mean(y, axis=-1, keepdims=True)
    yn = yc * lax.rsqrt(jnp.mean(yc * yc, axis=-1, keepdims=True) + EPS)
    yn = yn * lg_ref[...] + lb_ref[...]
    o_ref[0] = (yn * jax.nn.sigmoid(yn)).astype(o_ref.dtype)


def _conv_module(u, w_dw, b_dw, g_ln, b_ln, *, batch, seq, col_block, ts):
    cw = w_dw.shape[1]
    hb = ts // HALO_ROWS
    n_hblocks = seq // HALO_ROWS

    def main(c):
        return pl.BlockSpec((1, ts, cw), lambda b, i: (b, i, c))

    def prev(c):
        return pl.BlockSpec((1, HALO_ROWS, cw), lambda b, i: (b, jnp.maximum(i * hb - 1, 0), c))

    def nxt(c):
        return pl.BlockSpec((1, HALO_ROWS, cw),
                            lambda b, i: (b, jnp.minimum((i + 1) * hb, n_hblocks - 1), c))

    row = pl.BlockSpec((1, cw), lambda b, i: (0, 0))
    return pl.pallas_call(
        functools.partial(_conv_kernel, ts=ts),
        out_shape=jax.ShapeDtypeStruct((batch, seq, cw), BF16),
        grid=(batch, seq // ts),
        in_specs=[main(col_block), prev(col_block), nxt(col_block),
                  main(col_block + 1), prev(col_block + 1), nxt(col_block + 1),
                  pl.BlockSpec((CONV_KERNEL, cw), lambda b, i: (0, 0)), row, row, row],
        out_specs=pl.BlockSpec((1, ts, cw), lambda b, i: (b, i, 0)),
        scratch_shapes=[pltpu.VMEM((ts + 2 * HALO_ROWS, cw), F32)],
        compiler_params=_params("arbitrary", "arbitrary"),
        name="conv_module",
    )(u, u, u, u, u, u, w_dw, b_dw.reshape(1, cw), g_ln.reshape(1, cw), b_ln.reshape(1, cw))


def _cross_attn_kernel(q_ref, k_ref, v_ref, o_ref, *, head_dim):
    scale = head_dim ** -0.5
    for h in range(N_CROSS_HEADS):
        cols = slice(h * head_dim, (h + 1) * head_dim)
        s = lax.dot_general(q_ref[:, cols], k_ref[:, cols], (((1,), (1,)), ((), ())),
                            preferred_element_type=F32) * scale
        e = jnp.exp(s - jnp.max(s, axis=-1, keepdims=True))
        p = e * (1.0 / jnp.sum(e, axis=-1, keepdims=True))
        o_ref[:, cols] = jnp.dot(p.astype(BF16), v_ref[:, cols],
                                 preferred_element_type=F32).astype(o_ref.dtype)


def _cross_attention(q, k, v, *, batch, seq, n_mem, tm):
    width = q.shape[1]
    qb = seq // tm
    return pl.pallas_call(
        functools.partial(_cross_attn_kernel, head_dim=width // N_CROSS_HEADS),
        out_shape=jax.ShapeDtypeStruct(q.shape, BF16),
        grid=(batch, qb),
        in_specs=[pl.BlockSpec((tm, width), lambda b, i: (b * qb + i, 0)),
                  pl.BlockSpec((n_mem, width), lambda b, i: (b, 0)),
                  pl.BlockSpec((n_mem, width), lambda b, i: (b, 0))],
        out_specs=pl.BlockSpec((tm, width), lambda b, i: (b * qb + i, 0)),
        compiler_params=_params("arbitrary", "arbitrary"),
        name="cross_attention",
    )(q, k, v)


def _pack_bf16_pairs(lo_half, hi_half):
    a = pltpu.bitcast(lo_half, U32)
    b = pltpu.bitcast(hi_half, U32)
    return (a & jnp.uint32(0xFFFF0000)) | (b >> 16)


def _unpack_bf16_pairs(words):
    a = pltpu.bitcast(words & jnp.uint32(0xFFFF0000), F32)
    b = pltpu.bitcast(words << 16, F32)
    return a.astype(BF16), b.astype(BF16)


def _router_kernel(x_ref, g_ref, w_ref, b_ref, hp_ref, idx_ref, gate_ref, rank_ref, cnt_ref,
                   seen_ref):
    @pl.when(pl.program_id(0) == 0)
    def _():
        seen_ref[...] = jnp.zeros(seen_ref.shape, F32)

    x = x_ref[...]
    tm, d = x.shape
    y = x * lax.rsqrt(jnp.mean(x * x, axis=-1, keepdims=True) + EPS)
    hb = (y * g_ref[...]).astype(BF16)
    hp_ref[...] = _pack_bf16_pairs(hb[:, :d // 2].astype(F32), hb[:, d // 2:].astype(F32))
    logits = jnp.dot(hb, w_ref[...].astype(BF16), preferred_element_type=F32) + b_ref[...]
    n_exp = logits.shape[1]
    lane = lax.broadcasted_iota(jnp.int32, logits.shape, 1)
    kl = lax.broadcasted_iota(jnp.int32, idx_ref.shape, 1)
    vals = logits
    top_v = jnp.zeros(gate_ref.shape, F32)
    top_i = jnp.zeros(idx_ref.shape, jnp.int32)
    picks = []
    for kk in range(TOP_K):
        mx = jnp.max(vals, axis=-1, keepdims=True)
        sel = jnp.min(jnp.where(vals == mx, lane, n_exp), axis=-1, keepdims=True)
        top_v = jnp.where(kl == kk, mx, top_v)
        top_i = jnp.where(kl == kk, sel, top_i)
        picks.append(lane == sel)
        vals = jnp.where(picks[-1], -jnp.inf, vals)
    e = jnp.exp(top_v - jnp.max(top_v, axis=-1, keepdims=True))
    gate_ref[...] = e / jnp.sum(e, axis=-1, keepdims=True)
    idx_ref[...] = top_i
    member = jnp.zeros(logits.shape, F32)
    for pick in picks:
        member = member + pick.astype(F32)
    earlier = (lax.broadcasted_iota(jnp.int32, (tm, tm), 0)
               > lax.broadcasted_iota(jnp.int32, (tm, tm), 1)).astype(BF16)
    before = seen_ref[...] + jnp.dot(earlier, member.astype(BF16), preferred_element_type=F32)
    rank = jnp.zeros(rank_ref.shape, F32)
    for kk, pick in enumerate(picks):
        rank = jnp.where(kl == kk, jnp.sum(jnp.where(pick, before, 0.0), axis=-1, keepdims=True),
                         rank)
    rank_ref[...] = rank.astype(jnp.int32)
    seen_ref[...] = seen_ref[...] + jnp.sum(member, axis=0, keepdims=True)
    cnt_ref[...] = seen_ref[...].astype(jnp.int32)


def _router(x, g, w_router, b_router, layer, *, tm):
    n, d = x.shape
    n_exp = w_router.shape[2]
    per_tok = pl.BlockSpec((tm, TOP_K), lambda i: (i, 0))
    return pl.pallas_call(
        _router_kernel,
        out_shape=(jax.ShapeDtypeStruct((n, d // 2), U32),
                   jax.ShapeDtypeStruct((n, TOP_K), jnp.int32),
                   jax.ShapeDtypeStruct((n, TOP_K), F32),
                   jax.ShapeDtypeStruct((n, TOP_K), jnp.int32),
                   jax.ShapeDtypeStruct((1, n_exp), jnp.int32)),
        grid=(n // tm,),
        in_specs=[pl.BlockSpec((tm, d), lambda i: (i, 0)),
                  pl.BlockSpec((1, d), lambda i: (0, 0)),
                  pl.BlockSpec((None, d, n_exp), lambda i: (layer, 0, 0)),
                  pl.BlockSpec((1, n_exp), lambda i: (0, 0))],
        out_specs=(pl.BlockSpec((tm, d // 2), lambda i: (i, 0)), per_tok, per_tok, per_tok,
                   pl.BlockSpec((1, n_exp), lambda i: (0, 0))),
        scratch_shapes=[pltpu.VMEM((1, n_exp), F32)],
        compiler_params=_params("arbitrary"),
        name="router",
    )(x, g.reshape(1, d), w_router, b_router.reshape(1, n_exp))


def _dispatch_kernel(zt_ref, dest_ref, src_ref, dst_ref, zero_ref, zsem, sem, *, tt, tm):
    @pl.when(pl.program_id(0) == 0)
    def _():
        zero_ref[...] = jnp.zeros(zero_ref.shape, zero_ref.dtype)

        def zero_copy(i):
            return pltpu.make_async_copy(zero_ref, dst_ref.at[pl.ds(zt_ref[i] * tm, tm)], zsem.at[0])

        def start(i, carry):
            @pl.when(zt_ref[i] >= 0)
            def _():
                zero_copy(i).start()
            return carry

        def wait(i, carry):
            @pl.when(zt_ref[i] >= 0)
            def _():
                zero_copy(i).wait()
            return carry

        lax.fori_loop(0, zt_ref.shape[0], start, 0)
        lax.fori_loop(0, zt_ref.shape[0], wait, 0)

    def row_copy(r, kk):
        return pltpu.make_async_copy(src_ref.at[pl.ds(r, 1)],
                                     dst_ref.at[pl.ds(dest_ref[0, 0, r * TOP_K + kk], 1)],
                                     sem.at[0])

    def issue(r, carry):
        for kk in range(TOP_K):
            row_copy(r, kk).start()
        return carry

    def drain(r, carry):
        for kk in range(TOP_K):
            row_copy(r, kk).wait()
        return carry

    lax.fori_loop(0, tt, issue, 0, unroll=4)
    lax.fori_loop(0, tt, drain, 0, unroll=4)


def _dispatch(src, dest, zero_tiles, *, p, tt, tm):
    n, half = src.shape
    nt = n // tt
    return pl.pallas_call(
        functools.partial(_dispatch_kernel, tt=tt, tm=tm),
        out_shape=jax.ShapeDtypeStruct((p, half), src.dtype),
        grid_spec=pltpu.PrefetchScalarGridSpec(
            num_scalar_prefetch=1,
            grid=(nt,),
            in_specs=[pl.BlockSpec((1, 1, tt * TOP_K), lambda t, zt: (t, 0, 0),
                                   memory_space=pltpu.SMEM),
                      pl.BlockSpec((tt, half), lambda t, zt: (t, 0))],
            out_specs=pl.BlockSpec(memory_space=pl.ANY),
            scratch_shapes=[pltpu.VMEM((tm, half), src.dtype),
                            pltpu.SemaphoreType.DMA((1,)), pltpu.SemaphoreType.DMA((1,))],
        ),
        compiler_params=_params("arbitrary"),
        name="moe_dispatch",
    )(zero_tiles, dest.reshape(nt, 1, tt * TOP_K), src)


def _is_first_tile_of_expert(te_ref, t):
    return jnp.logical_or(t == 0, te_ref[t] != te_ref[jnp.maximum(t - 1, 0)])


def _gmm1_kernel(te_ref, nu_ref, x_ref, wa_ref, wb_ref, ba_ref, bb_ref, o_ref, wa_bf, wb_bf):
    t = pl.program_id(1)

    @pl.when(t < nu_ref[0])
    def _():
        @pl.when(_is_first_tile_of_expert(te_ref, t))
        def _():
            wa_bf[...] = wa_ref[...].astype(BF16)
            wb_bf[...] = wb_ref[...].astype(BF16)

        x_lo, x_hi = _unpack_bf16_pairs(x_ref[...])
        half = x_lo.shape[1]

        def proj(w_bf, b_ref):
            return (jnp.dot(x_lo, w_bf[:half, :], preferred_element_type=F32)
                    + jnp.dot(x_hi, w_bf[half:, :], preferred_element_type=F32) + b_ref[...])

        a = jnp.minimum(proj(wa_bf, ba_ref), SWIGLU_LIMIT)
        b = jnp.clip(proj(wb_bf, bb_ref), -SWIGLU_LIMIT, SWIGLU_LIMIT)
        o_ref[...] = (a * jax.nn.sigmoid(SWIGLU_ALPHA * a) * (b + 1.0)).astype(o_ref.dtype)


def _gmm1(xg, w1, b1, layer, tile_e, n_used, *, tm, tf):
    p, half = xg.shape
    d = 2 * half
    n_exp, ff = w1.shape[1], w1.shape[3] // 2
    nj = ff // tf
    assert ff % tf == 0 and p % tm == 0

    def row(t, nu):
        return jnp.minimum(t, nu[0] - 1)

    return pl.pallas_call(
        _gmm1_kernel,
        out_shape=jax.ShapeDtypeStruct((p, ff), BF16),
        grid_spec=pltpu.PrefetchScalarGridSpec(
            num_scalar_prefetch=2,
            grid=(nj, p // tm),
            in_specs=[
                pl.BlockSpec((tm, half), lambda j, t, te, nu: (row(t, nu), 0)),
                pl.BlockSpec((None, None, d, tf), lambda j, t, te, nu: (layer, te[t], 0, j)),
                pl.BlockSpec((None, None, d, tf), lambda j, t, te, nu: (layer, te[t], 0, nj + j)),
                pl.BlockSpec((None, None, 1, tf), lambda j, t, te, nu: (layer, te[t], 0, j)),
                pl.BlockSpec((None, None, 1, tf), lambda j, t, te, nu: (layer, te[t], 0, nj + j)),
            ],
            out_specs=pl.BlockSpec((tm, tf), lambda j, t, te, nu: (row(t, nu), j)),
            scratch_shapes=[pltpu.VMEM((d, tf), BF16), pltpu.VMEM((d, tf), BF16)],
        ),
        compiler_params=_params("arbitrary", "arbitrary"),
        name="moe_gmm1",
    )(tile_e, n_used, xg, w1, w1, b1.reshape(b1.shape[0], n_exp, 1, 2 * ff),
      b1.reshape(b1.shape[0], n_exp, 1, 2 * ff))


def _gmm2_kernel(te_ref, nu_ref, x_ref, w_ref, b_ref, o_ref, w_bf):
    t = pl.program_id(1)

    @pl.when(t < nu_ref[0])
    def _():
        @pl.when(_is_first_tile_of_expert(te_ref, t))
        def _():
            w_bf[...] = w_ref[...].astype(BF16)

        o_ref[...] = jnp.dot(x_ref[...], w_bf[...], preferred_element_type=F32) + b_ref[...]


def _gmm2(act, w2, b2, layer, tile_e, n_used, *, tm, tn):
    p, ff = act.shape
    n_exp, d = w2.shape[1], w2.shape[3]
    assert d % tn == 0

    def row(t, nu):
        return jnp.minimum(t, nu[0] - 1)

    return pl.pallas_call(
        _gmm2_kernel,
        out_shape=jax.ShapeDtypeStruct((p, d), F32),
        grid_spec=pltpu.PrefetchScalarGridSpec(
            num_scalar_prefetch=2,
            grid=(d // tn, p // tm),
            in_specs=[
                pl.BlockSpec((tm, ff), lambda j, t, te, nu: (row(t, nu), 0)),
                pl.BlockSpec((None, None, ff, tn), lambda j, t, te, nu: (layer, te[t], 0, j)),
                pl.BlockSpec((None, None, 1, tn), lambda j, t, te, nu: (layer, te[t], 0, j)),
            ],
            out_specs=pl.BlockSpec((tm, tn), lambda j, t, te, nu: (row(t, nu), j)),
            scratch_shapes=[pltpu.VMEM((ff, tn), BF16)],
        ),
        compiler_params=_params("arbitrary", "arbitrary"),
        name="moe_gmm2",
    )(tile_e, n_used, act, w2, b2.reshape(b2.shape[0], n_exp, 1, d))


def _combine_kernel(dcur_ref, dnext_ref, x_ref, gate_ref, g_ref, y_ref, o_ref, buf, sem, *,
                    tt, final_norm):
    t = pl.program_id(0)
    slot = lax.rem(t, 2)

    def row_copy(d_ref, r, kk, s):
        return pltpu.make_async_copy(y_ref.at[pl.ds(d_ref[0, 0, r * TOP_K + kk], 1)],
                                     buf.at[s, kk, pl.ds(r, 1)], sem.at[s])

    def issue(d_ref, s):
        def body(r, carry):
            for kk in range(TOP_K):
                row_copy(d_ref, r, kk, s).start()
            return carry
        lax.fori_loop(0, tt, body, 0, unroll=4)

    @pl.when(t == 0)
    def _():
        issue(dcur_ref, 0)

    @pl.when(t + 1 < pl.num_programs(0))
    def _():
        issue(dnext_ref, 1 - slot)

    def drain(r, carry):
        for kk in range(TOP_K):
            row_copy(dcur_ref, r, kk, slot).wait()
        return carry
    lax.fori_loop(0, tt, drain, 0, unroll=4)

    gate = gate_ref[...]
    acc = x_ref[...]
    for kk in range(TOP_K):
        acc = acc + gate[:, kk:kk + 1] * buf[slot, kk]
    if final_norm:
        acc = acc * lax.rsqrt(jnp.mean(acc * acc, axis=-1, keepdims=True) + EPS) * g_ref[...]
    o_ref[...] = acc


def _combine(x, y, dest, gate, g_final, *, tt, final_norm):
    n, d = x.shape
    nt = n // tt
    dest3 = dest.reshape(nt, 1, tt * TOP_K)
    return pl.pallas_call(
        functools.partial(_combine_kernel, tt=tt, final_norm=final_norm),
        out_shape=jax.ShapeDtypeStruct((n, d), F32),
        grid=(nt,),
        in_specs=[
            pl.BlockSpec((1, 1, tt * TOP_K), lambda t: (t, 0, 0), memory_space=pltpu.SMEM),
            pl.BlockSpec((1, 1, tt * TOP_K), lambda t: (jnp.minimum(t + 1, nt - 1), 0, 0),
                         memory_space=pltpu.SMEM),
            pl.BlockSpec((tt, d), lambda t: (t, 0)),
            pl.BlockSpec((tt, TOP_K), lambda t: (t, 0)),
            pl.BlockSpec((1, d), lambda t: (0, 0)),
            pl.BlockSpec(memory_space=pl.ANY),
        ],
        out_specs=pl.BlockSpec((tt, d), lambda t: (t, 0)),
        scratch_shapes=[pltpu.VMEM((2, TOP_K, tt, d), F32), pltpu.SemaphoreType.DMA((2,))],
        compiler_params=_params("arbitrary"),
        name="moe_combine",
    )(dest3, dest3, x, gate, g_final.reshape(1, d), y)


def _routing_tables(top_idx, rank, counts, tm):
    n = top_idx.shape[0]
    n_exp = counts.shape[0]
    n_tiles = (n * TOP_K) // tm + n_exp
    padded = (counts + tm - 1) // tm * tm
    pad_end = jnp.cumsum(padded)
    pad_start = pad_end - padded
    experts = jnp.arange(n_exp, dtype=jnp.int32)
    dest = rank + jnp.sum(jnp.where(top_idx[..., None] == experts, pad_start, 0), axis=-1)
    tile_start = jnp.arange(n_tiles, dtype=jnp.int32) * tm
    tile_e = jnp.minimum(jnp.sum(tile_start[:, None] >= pad_end[None, :], axis=-1),
                         n_exp - 1).astype(jnp.int32)
    n_used = pad_end[-1:] // tm
    ragged = jnp.where(counts % tm != 0, pad_end // tm - 1, -1)
    tail = n_used + experts
    zero_tiles = jnp.concatenate([ragged, jnp.where(tail < n_tiles, tail, -1)])
    return (dest.astype(jnp.int32), tile_e, n_used.astype(jnp.int32),
            zero_tiles.astype(jnp.int32), n_tiles * tm)


def _pick(n, pref):
    t = min(n, pref)
    while n % t or t % 8:
        t -= 1
    return t


def kernel(x, mem, g_mix, w_in, lambda_q1, lambda_k1, lambda_q2, lambda_k2, g_subln, w_dw, b_dw,
           g_conv_ln, b_conv_ln, w_out, rel_bias, g_cross, g_mem, w_cq, w_ck, w_cv, w_co, g_ffn,
           w_router, b_router, w1, b1, w2, b2, g_final):
    batch, seq, d = x.shape
    n = batch * seq
    n_mem = mem.shape[1]
    depth = g_mix.shape[0]
    n_heads = rel_bias.shape[1]
    attn_w = n_heads * 2 * DIFF_HEAD_DIM
    conv_w = w_dw.shape[2]
    in_w = w_in.shape[2]
    n_exp = w_router.shape[2]
    assert in_w == 3 * attn_w + 2 * conv_w and (3 * attn_w) % conv_w == 0

    tq = _pick(seq, 256)
    xf = x.reshape(n, d)
    memf = mem.reshape(batch * n_mem, d)
    for l in range(depth):
        lambda_init = 0.8 - 0.6 * math.exp(-0.3 * l)
        band, lam = _attn_prep(rel_bias, lambda_q1[l], lambda_k1[l], lambda_q2[l], lambda_k2[l],
                               tq=tq, lambda_init=lambda_init)
        h = _rmsnorm(xf, g_mix[l], rows=_pick(n, 256))
        u = _dense([h], w_in, l, tm=_pick(n, 512), tn=_pick(in_w, 1024), out_dtype=BF16,
                   name="in_proj").reshape(batch, seq, in_w)
        attn = _diff_attention(u, band, lam, rel_bias, g_subln[l], batch=batch, seq=seq,
                               n_heads=n_heads, tq=tq, lambda_init=lambda_init)
        conv = _conv_module(u, w_dw[l], b_dw[l], g_conv_ln[l], b_conv_ln[l], batch=batch, seq=seq,
                            col_block=3 * attn_w // conv_w, ts=_pick(seq, 256))
        xf = _dense([attn.reshape(n, attn_w), conv.reshape(n, conv_w)], w_out, l,
                    tm=_pick(n, 512), tn=_pick(d, 512), out_dtype=F32, res=xf, name="out_proj")
        h = _rmsnorm(xf, g_cross[l], rows=_pick(n, 256))
        m = _rmsnorm(memf, g_mem[l], rows=_pick(batch * n_mem, 256))
        cross_w = w_cq.shape[2]
        qc = _dense([h], w_cq, l, tm=_pick(n, 512), tn=_pick(cross_w, 512), out_dtype=BF16,
                    name="cross_q")
        kc = _dense([m], w_ck, l, tm=_pick(batch * n_mem, 512), tn=_pick(cross_w, 512),
                    out_dtype=BF16, name="cross_k")
        vc = _dense([m], w_cv, l, tm=_pick(batch * n_mem, 512), tn=_pick(cross_w, 512),
                    out_dtype=BF16, name="cross_v")
        oc = _cross_attention(qc, kc, vc, batch=batch, seq=seq, n_mem=n_mem, tm=_pick(seq, 512))
        xf = _dense([oc], w_co, l, tm=_pick(n, 512), tn=_pick(d, 1024), out_dtype=F32, res=xf,
                    name="cross_o")
        tm = 256
        hp, top_idx, gate, rank, counts = _router(xf, g_ffn[l], w_router, b_router[l], l,
                                                  tm=_pick(n, 256))
        dest, tile_e, n_used, zero_tiles, p_rows = _routing_tables(top_idx, rank, counts[0], tm)
        xg = _dispatch(hp, dest, zero_tiles, p=p_rows, tt=_pick(n, 256), tm=tm)
        ff = w2.shape[2]
        act = _gmm1(xg, w1, b1, l, tile_e, n_used, tm=tm, tf=_pick(ff, 512))
        y = _gmm2(act, w2, b2, l, tile_e, n_used, tm=tm, tn=_pick(d, 2048))
        xf = _combine(xf, y, dest, gate, g_final, tt=_pick(n, 64), final_norm=l == depth - 1)
    return xf.reshape(batch, seq, d)
```

```python
import functools
import math

import jax
import jax.numpy as jnp
from jax import lax
from jax.experimental import pallas as pl
from jax.experimental.pallas import tpu as pltpu

F32 = jnp.float32
BF16 = jnp.bfloat16
U32 = jnp.uint32

DIFF_HEAD_DIM = 128
CONV_KERNEL = 31
N_BUCKETS = 32
MAX_DISTANCE = 128
N_CROSS_HEADS = 4
TOP_K = 4
SWIGLU_LIMIT = 7.0
SWIGLU_ALPHA = 1.702
EPS = 1e-6
LOG2_E = math.log2(math.e)
N_BIAS_TILES = 5

V7X_VMEM_BYTES = 64 * 1024 * 1024
VMEM_LIMIT_BYTES = V7X_VMEM_BYTES - 6 * 1024 * 1024
LANES = 128
HALO_ROWS = 16

_T5_LOG_THRESHOLDS = tuple(
    math.ceil(8 * (MAX_DISTANCE / 8) ** (k / 8) - 1e-9) for k in range(1, 8))


def _params(*semantics):
    return pltpu.CompilerParams(dimension_semantics=semantics,
                                vmem_limit_bytes=VMEM_LIMIT_BYTES)


def _rmsnorm_kernel(x_ref, g_ref, o_ref):
    x = x_ref[...]
    y = x * lax.rsqrt(jnp.mean(x * x, axis=-1, keepdims=True) + EPS)
    o_ref[...] = (y * g_ref[...]).astype(o_ref.dtype)


def _rmsnorm(x, g, *, rows):
    n, d = x.shape
    return pl.pallas_call(
        _rmsnorm_kernel,
        out_shape=jax.ShapeDtypeStruct((n, d), BF16),
        grid=(n // rows,),
        in_specs=[pl.BlockSpec((rows, d), lambda i: (i, 0)),
                  pl.BlockSpec((1, d), lambda i: (0, 0))],
        out_specs=pl.BlockSpec((rows, d), lambda i: (i, 0)),
        compiler_params=_params("arbitrary"),
        name="rmsnorm",
    )(x, g.reshape(1, d))


def _dense_kernel(*refs, k_splits, has_res):
    n_lhs = len(k_splits)
    x_refs = refs[:n_lhs]
    w_ref = refs[n_lhs]
    res_ref = refs[n_lhs + 1] if has_res else None
    o_ref, wbf_ref = refs[-2], refs[-1]

    @pl.when(pl.program_id(1) == 0)
    def _():
        wbf_ref[...] = w_ref[...].astype(BF16)

    acc = None
    k0 = 0
    for x_ref, kw in zip(x_refs, k_splits):
        part = jnp.dot(x_ref[...], wbf_ref[k0:k0 + kw, :], preferred_element_type=F32)
        acc = part if acc is None else acc + part
        k0 += kw
    if has_res:
        acc = res_ref[...] + acc
    o_ref[...] = acc.astype(o_ref.dtype)


def _dense(xs, w, layer, *, tm, tn, out_dtype, res=None, name):
    m = xs[0].shape[0]
    k_splits = tuple(x.shape[1] for x in xs)
    k, n = w.shape[1], w.shape[2]
    assert sum(k_splits) == k and m % tm == 0 and n % tn == 0
    in_specs = [pl.BlockSpec((tm, kw), lambda j, i: (i, 0)) for kw in k_splits]
    in_specs.append(pl.BlockSpec((None, k, tn), lambda j, i: (layer, 0, j)))
    args = list(xs) + [w]
    if res is not None:
        in_specs.append(pl.BlockSpec((tm, tn), lambda j, i: (i, j)))
        args.append(res)
    return pl.pallas_call(
        functools.partial(_dense_kernel, k_splits=k_splits, has_res=res is not None),
        out_shape=jax.ShapeDtypeStruct((m, n), out_dtype),
        grid=(n // tn, m // tm),
        in_specs=in_specs,
        out_specs=pl.BlockSpec((tm, tn), lambda j, i: (i, j)),
        scratch_shapes=[pltpu.VMEM((k, tn), BF16)],
        compiler_params=_params("arbitrary", "arbitrary"),
        name=name,
    )(*args)


def _t5_bucket(rel):
    half = N_BUCKETS // 2
    max_exact = half // 2
    n = jnp.abs(rel)
    large = jnp.full(rel.shape, max_exact, jnp.int32)
    for thr in _T5_LOG_THRESHOLDS:
        large = large + (n >= thr).astype(jnp.int32)
    return jnp.where(rel > 0, half, 0) + jnp.where(n < max_exact, n, large)


def _attn_prep_kernel(rb_ref, lq1_ref, lk1_ref, lq2_ref, lk2_ref, tbl_ref, lam_ref, *,
                      tb, lambda_init):
    h = pl.program_id(0)
    u = pl.program_id(1)
    a = lax.broadcasted_iota(jnp.int32, (tb, tb), 0)
    b = lax.broadcasted_iota(jnp.int32, (tb, tb), 1)
    bucket = _t5_bucket((u - N_BIAS_TILES // 2) * tb + b - a)
    val = jnp.zeros((tb, tb), F32)
    for bk in range(N_BUCKETS):
        val = jnp.where(bucket == bk, rb_ref[bk, h], val)
    tbl_ref[0, 0] = val * LOG2_E
    s1 = jnp.sum(lq1_ref[...] * lk1_ref[...], axis=-1, keepdims=True)
    s2 = jnp.sum(lq2_ref[...] * lk2_ref[...], axis=-1, keepdims=True)
    lam_ref[...] = jnp.exp(s1) - jnp.exp(s2) + lambda_init


def _attn_prep(rel_bias, lq1, lk1, lq2, lk2, *, tb, lambda_init):
    n_heads = rel_bias.shape[1]
    assert tb > MAX_DISTANCE
    vec = pl.BlockSpec((1, DIFF_HEAD_DIM), lambda h, u: (0, 0))
    return pl.pallas_call(
        functools.partial(_attn_prep_kernel, tb=tb, lambda_init=lambda_init),
        out_shape=(jax.ShapeDtypeStruct((n_heads, N_BIAS_TILES, tb, tb), F32),
                   jax.ShapeDtypeStruct((1, 1), F32)),
        grid=(n_heads, N_BIAS_TILES),
        in_specs=[pl.BlockSpec(memory_space=pltpu.SMEM), vec, vec, vec, vec],
        out_specs=(pl.BlockSpec((1, 1, tb, tb), lambda h, u: (h, u, 0, 0)),
                   pl.BlockSpec((1, 1), lambda h, u: (0, 0))),
        compiler_params=_params("arbitrary", "arbitrary"),
        name="attn_prep",
    )(rel_bias, lq1.reshape(1, -1), lk1.reshape(1, -1), lq2.reshape(1, -1), lk2.reshape(1, -1))


def _diff_attn_kernel(lam_ref, q_ref, k_ref, v_ref, tbl_ref, g_ref, o_ref, *,
                      tq, tk, seq, lambda_init):
    i = pl.program_id(2)
    dh = DIFF_HEAD_DIM
    scale = dh ** -0.5 * LOG2_E
    q = q_ref[0]
    first_half = lax.broadcasted_iota(jnp.int32, q.shape, 1) < dh
    zero = jnp.zeros(q.shape, q.dtype)
    qq = jnp.concatenate([jnp.where(first_half, q, zero), jnp.where(first_half, zero, q)], axis=0)
    tiles_per_chunk = tk // tq
    parts = []
    for c in range(seq // tk):
        keys = slice(c * tk, (c + 1) * tk)
        s = lax.dot_general(qq, k_ref[0, keys, :], (((1,), (1,)), ((), ())),
                            preferred_element_type=F32)
        bias = jnp.concatenate(
            [tbl_ref[0, jnp.clip(c * tiles_per_chunk + j - i + N_BIAS_TILES // 2,
                                 0, N_BIAS_TILES - 1)] for j in range(tiles_per_chunk)], axis=1)
        t = s * scale + jnp.concatenate([bias, bias], axis=0)
        m = jnp.max(t, axis=-1, keepdims=True)
        e = jnp.exp2(t - m)
        parts.append((m, jnp.sum(e, axis=-1, keepdims=True),
                      jnp.dot(e.astype(BF16), v_ref[0, keys, :], preferred_element_type=F32)))
    m_all = parts[0][0]
    for m, _, _ in parts[1:]:
        m_all = jnp.maximum(m_all, m)
    l_all = jnp.zeros_like(m_all)
    o = jnp.zeros(parts[0][2].shape, F32)
    for m, l, oc in parts:
        w = jnp.exp2(m - m_all)
        l_all = l_all + w * l
        o = o + w * oc
    o = o * (1.0 / l_all)
    o = o[:tq] - lam_ref[0, 0] * o[tq:]
    y = o * lax.rsqrt(jnp.mean(o * o, axis=-1, keepdims=True) + EPS)
    o_ref[0] = ((y * g_ref[...]) * (1.0 - lambda_init)).astype(o_ref.dtype)


def _diff_attention(u, tbl, lam, g_subln, *, batch, seq, n_heads, tq, tk, lambda_init):
    hw = 2 * DIFF_HEAD_DIM
    assert tbl.shape[2] == tq and tk % tq == 0 and seq % tk == 0
    return pl.pallas_call(
        functools.partial(_diff_attn_kernel, tq=tq, tk=tk, seq=seq, lambda_init=lambda_init),
        out_shape=jax.ShapeDtypeStruct((batch, seq, n_heads * hw), BF16),
        grid=(batch, n_heads, seq // tq),
        in_specs=[
            pl.BlockSpec(memory_space=pltpu.SMEM),
            pl.BlockSpec((1, tq, hw), lambda b, h, i: (b, i, h)),
            pl.BlockSpec((1, seq, hw), lambda b, h, i: (b, 0, n_heads + h)),
            pl.BlockSpec((1, seq, hw), lambda b, h, i: (b, 0, 2 * n_heads + h)),
            pl.BlockSpec((1, N_BIAS_TILES, tq, tq), lambda b, h, i: (h, 0, 0, 0)),
            pl.BlockSpec((1, hw), lambda b, h, i: (0, 0)),
        ],
        out_specs=pl.BlockSpec((1, tq, hw), lambda b, h, i: (b, i, h)),
        compiler_params=_params("arbitrary", "arbitrary", "arbitrary"),
        name="diff_attention",
    )(lam, u, u, u, tbl, g_subln.reshape(1, hw))


def _conv_kernel(a_ref, ap_ref, an_ref, g_ref, gp_ref, gn_ref, w_ref, b_ref, lg_ref, lb_ref,
                 o_ref, z_ref, *, ts):
    i = pl.program_id(1)
    last = pl.num_programs(1) - 1

    def glu(a, g):
        return a[0].astype(F32) * jax.nn.sigmoid(g[0].astype(F32))

    z_ref[:HALO_ROWS, :] = glu(ap_ref, gp_ref) * (i > 0).astype(F32)
    z_ref[HALO_ROWS:HALO_ROWS + ts, :] = glu(a_ref, g_ref)
    z_ref[HALO_ROWS + ts:, :] = glu(an_ref, gn_ref) * (i < last).astype(F32)
    first = HALO_ROWS - CONV_KERNEL // 2
    acc = jnp.zeros(o_ref.shape[1:], F32)
    for t in range(CONV_KERNEL):
        acc = acc + w_ref[t:t + 1, :] * z_ref[first + t:first + t + ts, :]
    y = acc + b_ref[...]
    yc = y - jnp.mean(y, axis=-1, keepdims=True)
    yn = yc * lax.rsqrt(jnp.mean(yc * yc, axis=-1, keepdims=True) + EPS)
    yn = yn * lg_ref[...] + lb_ref[...]
    o_ref[0] = (yn * jax.nn.sigmoid(yn)).astype(o_ref.dtype)


def _conv_module(u, w_dw, b_dw, g_ln, b_ln, *, batch, seq, col_block, ts):
    cw = w_dw.shape[1]
    hb = ts // HALO_ROWS
    n_hblocks = seq // HALO_ROWS

    def main(c):
        return pl.BlockSpec((1, ts, cw), lambda b, i: (b, i, c))

    def prev(c):
        return pl.BlockSpec((1, HALO_ROWS, cw), lambda b, i: (b, jnp.maximum(i * hb - 1, 0), c))

    def nxt(c):
        return pl.BlockSpec((1, HALO_ROWS, cw),
                            lambda b, i: (b, jnp.minimum((i + 1) * hb, n_hblocks - 1), c))

    row = pl.BlockSpec((1, cw), lambda b, i: (0, 0))
    return pl.pallas_call(
        functools.partial(_conv_kernel, ts=ts),
        out_shape=jax.ShapeDtypeStruct((batch, seq, cw), BF16),
        grid=(batch, seq // ts),
        in_specs=[main(col_block), prev(col_block), nxt(col_block),
                  main(col_block + 1), prev(col_block + 1), nxt(col_block + 1),
                  pl.BlockSpec((CONV_KERNEL, cw), lambda b, i: (0, 0)), row, row, row],
        out_specs=pl.BlockSpec((1, ts, cw), lambda b, i: (b, i, 0)),
        scratch_shapes=[pltpu.VMEM((ts + 2 * HALO_ROWS, cw), F32)],
        compiler_params=_params("arbitrary", "arbitrary"),
        name="conv_module",
    )(u, u, u, u, u, u, w_dw, b_dw.reshape(1, cw), g_ln.reshape(1, cw), b_ln.reshape(1, cw))


def _cross_attn_kernel(q_ref, k_ref, v_ref, o_ref, *, head_dim):
    scale = head_dim ** -0.5
    for h in range(N_CROSS_HEADS):
        cols = slice(h * head_dim, (h + 1) * head_dim)
        s = lax.dot_general(q_ref[:, cols], k_ref[:, cols], (((1,), (1,)), ((), ())),
                            preferred_element_type=F32) * scale
        e = jnp.exp(s - jnp.max(s, axis=-1, keepdims=True))
        p = e * (1.0 / jnp.sum(e, axis=-1, keepdims=True))
        o_ref[:, cols] = jnp.dot(p.astype(BF16), v_ref[:, cols],
                                 preferred_element_type=F32).astype(o_ref.dtype)


def _cross_attention(q, k, v, *, batch, seq, n_mem, tm):
    width = q.shape[1]
    qb = seq // tm
    return pl.pallas_call(
        functools.partial(_cross_attn_kernel, head_dim=width // N_CROSS_HEADS),
        out_shape=jax.ShapeDtypeStruct(q.shape, BF16),
        grid=(batch, qb),
        in_specs=[pl.BlockSpec((tm, width), lambda b, i: (b * qb + i, 0)),
                  pl.BlockSpec((n_mem, width), lambda b, i: (b, 0)),
                  pl.BlockSpec((n_mem, width), lambda b, i: (b, 0))],
        out_specs=pl.BlockSpec((tm, width), lambda b, i: (b * qb + i, 0)),
        compiler_params=_params("arbitrary", "arbitrary"),
        name="cross_attention",
    )(q, k, v)


def _pack_bf16_pairs(lo_half, hi_half):
    a = pltpu.bitcast(lo_half, U32)
    b = pltpu.bitcast(hi_half, U32)
    return (a & jnp.uint32(0xFFFF0000)) | (b >> 16)


def _unpack_bf16_pairs(words):
    a = pltpu.bitcast(words & jnp.uint32(0xFFFF0000), F32)
    b = pltpu.bitcast(words << 16, F32)
    return a.astype(BF16), b.astype(BF16)


def _router_kernel(x_ref, g_ref, w_ref, b_ref, hp_ref, idx_ref, gate_ref, rank_ref, cnt_ref,
                   seen_ref):
    @pl.when(pl.program_id(0) == 0)
    def _():
        seen_ref[...] = jnp.zeros(seen_ref.shape, F32)

    x = x_ref[...]
    tm, d = x.shape
    y = x * lax.rsqrt(jnp.mean(x * x, axis=-1, keepdims=True) + EPS)
    hb = (y * g_ref[...]).astype(BF16)
    hp_ref[...] = _pack_bf16_pairs(hb[:, :d // 2].astype(F32), hb[:, d // 2:].astype(F32))
    logits = jnp.dot(hb, w_ref[...].astype(BF16), preferred_element_type=F32) + b_ref[...]
    n_exp = logits.shape[1]
    lane = lax.broadcasted_iota(jnp.int32, logits.shape, 1)
    kl = lax.broadcasted_iota(jnp.int32, idx_ref.shape, 1)
    vals = logits
    top_v = jnp.zeros(gate_ref.shape, F32)
    top_i = jnp.zeros(idx_ref.shape, jnp.int32)
    picks = []
    for kk in range(TOP_K):
        mx = jnp.max(vals, axis=-1, keepdims=True)
        sel = jnp.min(jnp.where(vals == mx, lane, n_exp), axis=-1, keepdims=True)
        top_v = jnp.where(kl == kk, mx, top_v)
        top_i = jnp.where(kl == kk, sel, top_i)
        picks.append(lane == sel)
        vals = jnp.where(picks[-1], -jnp.inf, vals)
    e = jnp.exp(top_v - jnp.max(top_v, axis=-1, keepdims=True))
    gate_ref[...] = e / jnp.sum(e, axis=-1, keepdims=True)
    idx_ref[...] = top_i
    member = jnp.zeros(logits.shape, F32)
    for pick in picks:
        member = member + pick.astype(F32)
    earlier = (lax.broadcasted_iota(jnp.int32, (tm, tm), 0)
               > lax.broadcasted_iota(jnp.int32, (tm, tm), 1)).astype(BF16)
    before = seen_ref[...] + jnp.dot(earlier, member.astype(BF16), preferred_element_type=F32)
    rank = jnp.zeros(rank_ref.shape, F32)
    for kk, pick in enumerate(picks):
        rank = jnp.where(kl == kk, jnp.sum(jnp.where(pick, before, 0.0), axis=-1, keepdims=True),
                         rank)
    rank_ref[...] = rank.astype(jnp.int32)
    seen_ref[...] = seen_ref[...] + jnp.sum(member, axis=0, keepdims=True)
    cnt_ref[...] = seen_ref[...].astype(jnp.int32)


def _router(x, g, w_router, b_router, layer, *, tm):
    n, d = x.shape
    n_exp = w_router.shape[2]
    per_tok = pl.BlockSpec((tm, TOP_K), lambda i: (i, 0))
    return pl.pallas_call(
        _router_kernel,
        out_shape=(jax.ShapeDtypeStruct((n, d // 2), U32),
                   jax.ShapeDtypeStruct((n, TOP_K), jnp.int32),
                   jax.ShapeDtypeStruct((n, TOP_K), F32),
                   jax.ShapeDtypeStruct((n, TOP_K), jnp.int32),
                   jax.ShapeDtypeStruct((1, n_exp), jnp.int32)),
        grid=(n // tm,),
        in_specs=[pl.BlockSpec((tm, d), lambda i: (i, 0)),
                  pl.BlockSpec((1, d), lambda i: (0, 0)),
                  pl.BlockSpec((None, d, n_exp), lambda i: (layer, 0, 0)),
                  pl.BlockSpec((1, n_exp), lambda i: (0, 0))],
        out_specs=(pl.BlockSpec((tm, d // 2), lambda i: (i, 0)), per_tok, per_tok, per_tok,
                   pl.BlockSpec((1, n_exp), lambda i: (0, 0))),
        scratch_shapes=[pltpu.VMEM((1, n_exp), F32)],
        compiler_params=_params("arbitrary"),
        name="router",
    )(x, g.reshape(1, d), w_router, b_router.reshape(1, n_exp))


def _dispatch_kernel(zt_ref, dest_ref, src_ref, dst_ref, zero_ref, zsem, sem, *, tt, tm):
    @pl.when(pl.program_id(0) == 0)
    def _():
        zero_ref[...] = jnp.zeros(zero_ref.shape, zero_ref.dtype)

        def zero_copy(i):
            return pltpu.make_async_copy(zero_ref, dst_ref.at[pl.ds(zt_ref[i] * tm, tm)], zsem.at[0])

        def start(i, carry):
            @pl.when(zt_ref[i] >= 0)
            def _():
                zero_copy(i).start()
            return carry

        def wait(i, carry):
            @pl.when(zt_ref[i] >= 0)
            def _():
                zero_copy(i).wait()
            return carry

        lax.fori_loop(0, zt_ref.shape[0], start, 0)
        lax.fori_loop(0, zt_ref.shape[0], wait, 0)

    def row_copy(r, kk):
        return pltpu.make_async_copy(src_ref.at[pl.ds(r, 1)],
                                     dst_ref.at[pl.ds(dest_ref[0, 0, r * TOP_K + kk], 1)],
                                     sem.at[0])

    def issue(r, carry):
        for kk in range(TOP_K):
            row_copy(r, kk).start()
        return carry

    def drain(r, carry):
        for kk in range(TOP_K):
            row_copy(r, kk).wait()
        return carry

    lax.fori_loop(0, tt, issue, 0, unroll=4)
    lax.fori_loop(0, tt, drain, 0, unroll=4)


def _dispatch(src, dest, zero_tiles, *, p, tt, tm):
    n, half = src.shape
    nt = n // tt
    return pl.pallas_call(
        functools.partial(_dispatch_kernel, tt=tt, tm=tm),
        out_shape=jax.ShapeDtypeStruct((p, half), src.dtype),
        grid_spec=pltpu.PrefetchScalarGridSpec(
            num_scalar_prefetch=1,
            grid=(nt,),
            in_specs=[pl.BlockSpec((1, 1, tt * TOP_K), lambda t, zt: (t, 0, 0),
                                   memory_space=pltpu.SMEM),
                      pl.BlockSpec((tt, half), lambda t, zt: (t, 0))],
            out_specs=pl.BlockSpec(memory_space=pl.ANY),
            scratch_shapes=[pltpu.VMEM((tm, half), src.dtype),
                            pltpu.SemaphoreType.DMA((1,)), pltpu.SemaphoreType.DMA((1,))],
        ),
        compiler_params=_params("arbitrary"),
        name="moe_dispatch",
    )(zero_tiles, dest.reshape(nt, 1, tt * TOP_K), src)


def _is_first_tile_of_expert(te_ref, t):
    return jnp.logical_or(t == 0, te_ref[t] != te_ref[jnp.maximum(t - 1, 0)])


def _gmm1_kernel(te_ref, nu_ref, x_ref, wa_ref, wb_ref, ba_ref, bb_ref, o_ref, wa_bf, wb_bf):
    t = pl.program_id(1)

    @pl.when(t < nu_ref[0])
    def _():
        @pl.when(_is_first_tile_of_expert(te_ref, t))
        def _():
            wa_bf[...] = wa_ref[...].astype(BF16)
            wb_bf[...] = wb_ref[...].astype(BF16)

        x_lo, x_hi = _unpack_bf16_pairs(x_ref[...])
        half = x_lo.shape[1]

        def proj(w_bf, b_ref):
            return (jnp.dot(x_lo, w_bf[:half, :], preferred_element_type=F32)
                    + jnp.dot(x_hi, w_bf[half:, :], preferred_element_type=F32) + b_ref[...])

        a = jnp.minimum(proj(wa_bf, ba_ref), SWIGLU_LIMIT)
        b = jnp.clip(proj(wb_bf, bb_ref), -SWIGLU_LIMIT, SWIGLU_LIMIT)
        o_ref[...] = (a * jax.nn.sigmoid(SWIGLU_ALPHA * a) * (b + 1.0)).astype(o_ref.dtype)


def _gmm1(xg, w1, b1, layer, tile_e, n_used, *, tm, tf):
    p, half = xg.shape
    d = 2 * half
    n_exp, ff = w1.shape[1], w1.shape[3] // 2
    nj = ff // tf
    assert ff % tf == 0 and p % tm == 0

    def row(t, nu):
        return jnp.minimum(t, nu[0] - 1)

    return pl.pallas_call(
        _gmm1_kernel,
        out_shape=jax.ShapeDtypeStruct((p, ff), BF16),
        grid_spec=pltpu.PrefetchScalarGridSpec(
            num_scalar_prefetch=2,
            grid=(nj, p // tm),
            in_specs=[
                pl.BlockSpec((tm, half), lambda j, t, te, nu: (row(t, nu), 0)),
                pl.BlockSpec((None, None, d, tf), lambda j, t, te, nu: (layer, te[t], 0, j)),
                pl.BlockSpec((None, None, d, tf), lambda j, t, te, nu: (layer, te[t], 0, nj + j)),
                pl.BlockSpec((None, None, 1, tf), lambda j, t, te, nu: (layer, te[t], 0, j)),
                pl.BlockSpec((None, None, 1, tf), lambda j, t, te, nu: (layer, te[t], 0, nj + j)),
            ],
            out_specs=pl.BlockSpec((tm, tf), lambda j, t, te, nu: (row(t, nu), j)),
            scratch_shapes=[pltpu.VMEM((d, tf), BF16), pltpu.VMEM((d, tf), BF16)],
        ),
        compiler_params=_params("arbitrary", "arbitrary"),
        name="moe_gmm1",
    )(tile_e, n_used, xg, w1, w1, b1.reshape(b1.shape[0], n_exp, 1, 2 * ff),
      b1.reshape(b1.shape[0], n_exp, 1, 2 * ff))


def _gmm2_kernel(te_ref, nu_ref, x_ref, w_ref, b_ref, o_ref, w_bf):
    t = pl.program_id(1)

    @pl.when(t < nu_ref[0])
    def _():
        @pl.when(_is_first_tile_of_expert(te_ref, t))
        def _():
            w_bf[...] = w_ref[...].astype(BF16)

        o_ref[...] = jnp.dot(x_ref[...], w_bf[...], preferred_element_type=F32) + b_ref[...]


def _gmm2(act, w2, b2, layer, tile_e, n_used, *, tm, tn):
    p, ff = act.shape
    n_exp, d = w2.shape[1], w2.shape[3]
    assert d % tn == 0

    def row(t, nu):
        return jnp.minimum(t, nu[0] - 1)

    return pl.pallas_call(
        _gmm2_kernel,
        out_shape=jax.ShapeDtypeStruct((p, d), F32),
        grid_spec=pltpu.PrefetchScalarGridSpec(
            num_scalar_prefetch=2,
            grid=(d // tn, p // tm),
            in_specs=[
                pl.BlockSpec((tm, ff), lambda j, t, te, nu: (row(t, nu), 0)),
                pl.BlockSpec((None, None, ff, tn), lambda j, t, te, nu: (layer, te[t], 0, j)),
                pl.BlockSpec((None, None, 1, tn), lambda j, t, te, nu: (layer, te[t], 0, j)),
            ],
            out_specs=pl.BlockSpec((tm, tn), lambda j, t, te, nu: (row(t, nu), j)),
            scratch_shapes=[pltpu.VMEM((ff, tn), BF16)],
        ),
        compiler_params=_params("arbitrary", "arbitrary"),
        name="moe_gmm2",
    )(tile_e, n_used, act, w2, b2.reshape(b2.shape[0], n_exp, 1, d))


def _combine_kernel(dcur_ref, dnext_ref, x_ref, gate_ref, g_ref, y_ref, o_ref, buf, sem, *,
                    tt, final_norm):
    t = pl.program_id(0)
    slot = lax.rem(t, 2)

    def row_copy(d_ref, r, kk, s):
        return pltpu.make_async_copy(y_ref.at[pl.ds(d_ref[0, 0, r * TOP_K + kk], 1)],
                                     buf.at[s, kk, pl.ds(r, 1)], sem.at[s])

    def issue(d_ref, s):
        def body(r, carry):
            for kk in range(TOP_K):
                row_copy(d_ref, r, kk, s).start()
            return carry
        lax.fori_loop(0, tt, body, 0, unroll=4)

    @pl.when(t == 0)
    def _():
        issue(dcur_ref, 0)

    @pl.when(t + 1 < pl.num_programs(0))
    def _():
        issue(dnext_ref, 1 - slot)

    def drain(r, carry):
        for kk in range(TOP_K):
            row_copy(dcur_ref, r, kk, slot).wait()
        return carry
    lax.fori_loop(0, tt, drain, 0, unroll=4)

    gate = gate_ref[...]
    acc = x_ref[...]
    for kk in range(TOP_K):
        acc = acc + gate[:, kk:kk + 1] * buf[slot, kk]
    if final_norm:
        acc = acc * lax.rsqrt(jnp.mean(acc * acc, axis=-1, keepdims=True) + EPS) * g_ref[...]
    o_ref[...] = acc


def _combine(x, y, dest, gate, g_final, *, tt, final_norm):
    n, d = x.shape
    nt = n // tt
    dest3 = dest.reshape(nt, 1, tt * TOP_K)
    return pl.pallas_call(
        functools.partial(_combine_kernel, tt=tt, final_norm=final_norm),
        out_shape=jax.ShapeDtypeStruct((n, d), F32),
        grid=(nt,),
        in_specs=[
            pl.BlockSpec((1, 1, tt * TOP_K), lambda t: (t, 0, 0), memory_space=pltpu.SMEM),
            pl.BlockSpec((1, 1, tt * TOP_K), lambda t: (jnp.minimum(t + 1, nt - 1), 0, 0),
                         memory_space=pltpu.SMEM),
            pl.BlockSpec((tt, d), lambda t: (t, 0)),
            pl.BlockSpec((tt, TOP_K), lambda t: (t, 0)),
            pl.BlockSpec((1, d), lambda t: (0, 0)),
            pl.BlockSpec(memory_space=pl.ANY),
        ],
        out_specs=pl.BlockSpec((tt, d), lambda t: (t, 0)),
        scratch_shapes=[pltpu.VMEM((2, TOP_K, tt, d), F32), pltpu.SemaphoreType.DMA((2,))],
        compiler_params=_params("arbitrary"),
        name="moe_combine",
    )(dest3, dest3, x, gate, g_final.reshape(1, d), y)


def _routing_tables(top_idx, rank, counts, tm):
    n = top_idx.shape[0]
    n_exp = counts.shape[0]
    n_tiles = (n * TOP_K) // tm + n_exp
    padded = (counts + tm - 1) // tm * tm
    pad_end = jnp.cumsum(padded)
    pad_start = pad_end - padded
    experts = jnp.arange(n_exp, dtype=jnp.int32)
    dest = rank + jnp.sum(jnp.where(top_idx[..., None] == experts, pad_start, 0), axis=-1)
    tile_start = jnp.arange(n_tiles, dtype=jnp.int32) * tm
    tile_e = jnp.minimum(jnp.sum(tile_start[:, None] >= pad_end[None, :], axis=-1),
                         n_exp - 1).astype(jnp.int32)
    n_used = pad_end[-1:] // tm
    ragged = jnp.where(counts % tm != 0, pad_end // tm - 1, -1)
    tail = n_used + experts
    zero_tiles = jnp.concatenate([ragged, jnp.where(tail < n_tiles, tail, -1)])
    return (dest.astype(jnp.int32), tile_e, n_used.astype(jnp.int32),
            zero_tiles.astype(jnp.int32), n_tiles * tm)


def _pick(n, pref):
    t = min(n, pref)
    while n % t or t % 8:
        t -= 1
    return t


def kernel(x, mem, g_mix, w_in, lambda_q1, lambda_k1, lambda_q2, lambda_k2, g_subln, w_dw, b_dw,
           g_conv_ln, b_conv_ln, w_out, rel_bias, g_cross, g_mem, w_cq, w_ck, w_cv, w_co, g_ffn,
           w_router, b_router, w1, b1, w2, b2, g_final):
    batch, seq, d = x.shape
    n = batch * seq
    n_mem = mem.shape[1]
    depth = g_mix.shape[0]
    n_heads = rel_bias.shape[1]
    attn_w = n_heads * 2 * DIFF_HEAD_DIM
    conv_w = w_dw.shape[2]
    in_w = w_in.shape[2]
    n_exp = w_router.shape[2]
    assert in_w == 3 * attn_w + 2 * conv_w and (3 * attn_w) % conv_w == 0

    tq = _pick(seq, 512)
    xf = x.reshape(n, d)
    memf = mem.reshape(batch * n_mem, d)
    for l in range(depth):
        lambda_init = 0.8 - 0.6 * math.exp(-0.3 * l)
        tbl, lam = _attn_prep(rel_bias, lambda_q1[l], lambda_k1[l], lambda_q2[l], lambda_k2[l],
                              tb=tq, lambda_init=lambda_init)
        h = _rmsnorm(xf, g_mix[l], rows=_pick(n, 256))
        u = _dense([h], w_in, l, tm=_pick(n, 512), tn=_pick(in_w, 1024), out_dtype=BF16,
                   name="in_proj").reshape(batch, seq, in_w)
        attn = _diff_attention(u, tbl, lam, g_subln[l], batch=batch, seq=seq, n_heads=n_heads,
                               tq=tq, tk=_pick(seq, 1024), lambda_init=lambda_init)
        conv = _conv_module(u, w_dw[l], b_dw[l], g_conv_ln[l], b_conv_ln[l], batch=batch, seq=seq,
                            col_block=3 * attn_w // conv_w, ts=_pick(seq, 256))
        xf = _dense([attn.reshape(n, attn_w), conv.reshape(n, conv_w)], w_out, l,
                    tm=_pick(n, 512), tn=_pick(d, 512), out_dtype=F32, res=xf, name="out_proj")
        h = _rmsnorm(xf, g_cross[l], rows=_pick(n, 256))
        m = _rmsnorm(memf, g_mem[l], rows=_pick(batch * n_mem, 256))
        cross_w = w_cq.shape[2]
        qc = _dense([h], w_cq, l, tm=_pick(n, 512), tn=_pick(cross_w, 512), out_dtype=BF16,
                    name="cross_q")
        kc = _dense([m], w_ck, l, tm=_pick(batch * n_mem, 512), tn=_pick(cross_w, 512),
                    out_dtype=BF16, name="cross_k")
        vc = _dense([m], w_cv, l, tm=_pick(batch * n_mem, 512), tn=_pick(cross_w, 512),
                    out_dtype=BF16, name="cross_v")
        oc = _cross_attention(qc, kc, vc, batch=batch, seq=seq, n_mem=n_mem, tm=_pick(seq, 512))
        xf = _dense([oc], w_co, l, tm=_pick(n, 512), tn=_pick(d, 2048), out_dtype=F32, res=xf,
                    name="cross_o")
        tm = 512
        hp, top_idx, gate, rank, counts = _router(xf, g_ffn[l], w_router, b_router[l], l,
                                                  tm=_pick(n, 256))
        dest, tile_e, n_used, zero_tiles, p_rows = _routing_tables(top_idx, rank, counts[0], tm)
        xg = _dispatch(hp, dest, zero_tiles, p=p_rows, tt=_pick(n, 256), tm=tm)
        ff = w2.shape[2]
        act = _gmm1(xg, w1, b1, l, tile_e, n_used, tm=tm, tf=_pick(ff, 512))
        y = _gmm2(act, w2, b2, l, tile_e, n_used, tm=tm, tn=_pick(d, 2048))
        xf = _combine(xf, y, dest, gate, g_final, tt=_pick(n, 64), final_norm=l == depth - 1)
    return xf.reshape(batch, seq, d)
```

```python
import functools
import math

import jax
import jax.numpy as jnp
from jax import lax
from jax.experimental import pallas as pl
from jax.experimental.pallas import tpu as pltpu

F32 = jnp.float32
BF16 = jnp.bfloat16
U32 = jnp.uint32

DIFF_HEAD_DIM = 128
CONV_KERNEL = 31
N_BUCKETS = 32
MAX_DISTANCE = 128
N_CROSS_HEADS = 4
TOP_K = 4
SWIGLU_LIMIT = 7.0
SWIGLU_ALPHA = 1.702
EPS = 1e-6
LOG2_E = math.log2(math.e)
N_BIAS_TILES = 5

V7X_VMEM_BYTES = 64 * 1024 * 1024
VMEM_LIMIT_BYTES = V7X_VMEM_BYTES - 6 * 1024 * 1024
LANES = 128
CONV_RBLOCK = 64
CONV_CBLOCK = 256
HALO_ROWS = 16

_T5_LOG_THRESHOLDS = tuple(
    math.ceil(8 * (MAX_DISTANCE / 8) ** (k / 8) - 1e-9) for k in range(1, 8))


def _params(*semantics):
    return pltpu.CompilerParams(dimension_semantics=semantics,
                                vmem_limit_bytes=VMEM_LIMIT_BYTES)


def _rmsnorm_kernel(x_ref, g_ref, o_ref):
    x = x_ref[...]
    y = x * lax.rsqrt(jnp.mean(x * x, axis=-1, keepdims=True) + EPS)
    o_ref[...] = (y * g_ref[...]).astype(o_ref.dtype)


def _rmsnorm(x, g, *, rows):
    n, d = x.shape
    return pl.pallas_call(
        _rmsnorm_kernel,
        out_shape=jax.ShapeDtypeStruct((n, d), BF16),
        grid=(n // rows,),
        in_specs=[pl.BlockSpec((rows, d), lambda i: (i, 0)),
                  pl.BlockSpec((1, d), lambda i: (0, 0))],
        out_specs=pl.BlockSpec((rows, d), lambda i: (i, 0)),
        compiler_params=_params("arbitrary"),
        name="rmsnorm",
    )(x, g.reshape(1, d))


def _dense_kernel(*refs, k_splits, has_res):
    n_lhs = len(k_splits)
    x_refs = refs[:n_lhs]
    w_ref = refs[n_lhs]
    res_ref = refs[n_lhs + 1] if has_res else None
    o_ref, wbf_ref = refs[-2], refs[-1]

    @pl.when(pl.program_id(1) == 0)
    def _():
        wbf_ref[...] = w_ref[...].astype(BF16)

    acc = None
    k0 = 0
    for x_ref, kw in zip(x_refs, k_splits):
        part = jnp.dot(x_ref[...], wbf_ref[k0:k0 + kw, :], preferred_element_type=F32)
        acc = part if acc is None else acc + part
        k0 += kw
    if has_res:
        acc = res_ref[...] + acc
    o_ref[...] = acc.astype(o_ref.dtype)


def _dense(xs, w, layer, *, tm, tn, out_dtype, res=None, name):
    m = xs[0].shape[0]
    k_splits = tuple(x.shape[1] for x in xs)
    k, n = w.shape[1], w.shape[2]
    assert sum(k_splits) == k and m % tm == 0 and n % tn == 0
    in_specs = [pl.BlockSpec((tm, kw), lambda j, i: (i, 0)) for kw in k_splits]
    in_specs.append(pl.BlockSpec((None, k, tn), lambda j, i: (layer, 0, j)))
    args = list(xs) + [w]
    if res is not None:
        in_specs.append(pl.BlockSpec((tm, tn), lambda j, i: (i, j)))
        args.append(res)
    return pl.pallas_call(
        functools.partial(_dense_kernel, k_splits=k_splits, has_res=res is not None),
        out_shape=jax.ShapeDtypeStruct((m, n), out_dtype),
        grid=(n // tn, m // tm),
        in_specs=in_specs,
        out_specs=pl.BlockSpec((tm, tn), lambda j, i: (i, j)),
        scratch_shapes=[pltpu.VMEM((k, tn), BF16)],
        compiler_params=_params("arbitrary", "arbitrary"),
        name=name,
    )(*args)


def _t5_bucket(rel):
    half = N_BUCKETS // 2
    max_exact = half // 2
    n = jnp.abs(rel)
    large = jnp.full(rel.shape, max_exact, jnp.int32)
    for thr in _T5_LOG_THRESHOLDS:
        large = large + (n >= thr).astype(jnp.int32)
    return jnp.where(rel > 0, half, 0) + jnp.where(n < max_exact, n, large)


def _t5_bucket_static(rel):
    half = N_BUCKETS // 2
    n = abs(rel)
    large = half // 2 + sum(n >= thr for thr in _T5_LOG_THRESHOLDS)
    return (half if rel > 0 else 0) + (n if n < half // 2 else large)


def _attn_prep_kernel(rb_ref, lq1_ref, lk1_ref, lq2_ref, lk2_ref, tbl_ref, lam_ref, *,
                      tb, lambda_init):
    h = pl.program_id(0)
    a = lax.broadcasted_iota(jnp.int32, (tb, tb), 0)
    b = lax.broadcasted_iota(jnp.int32, (tb, tb), 1)
    for u in range(N_BIAS_TILES):
        origin = (u - N_BIAS_TILES // 2) * tb
        reachable = sorted({_t5_bucket_static(rel)
                            for rel in range(origin - tb + 1, origin + tb)})
        bucket = _t5_bucket(origin + b - a)
        val = jnp.full((tb, tb), rb_ref[reachable[0], h], F32)
        for bk in reachable[1:]:
            val = jnp.where(bucket == bk, rb_ref[bk, h], val)
        tbl_ref[0, u] = val * LOG2_E
    s1 = jnp.sum(lq1_ref[...] * lk1_ref[...], axis=-1, keepdims=True)
    s2 = jnp.sum(lq2_ref[...] * lk2_ref[...], axis=-1, keepdims=True)
    lam_ref[...] = jnp.exp(s1) - jnp.exp(s2) + lambda_init


def _attn_prep(rel_bias, lq1, lk1, lq2, lk2, *, tb, lambda_init):
    n_heads = rel_bias.shape[1]
    assert tb > MAX_DISTANCE
    vec = pl.BlockSpec((1, DIFF_HEAD_DIM), lambda h: (0, 0))
    return pl.pallas_call(
        functools.partial(_attn_prep_kernel, tb=tb, lambda_init=lambda_init),
        out_shape=(jax.ShapeDtypeStruct((n_heads, N_BIAS_TILES, tb, tb), F32),
                   jax.ShapeDtypeStruct((1, 1), F32)),
        grid=(n_heads,),
        in_specs=[pl.BlockSpec(memory_space=pltpu.SMEM), vec, vec, vec, vec],
        out_specs=(pl.BlockSpec((1, N_BIAS_TILES, tb, tb), lambda h: (h, 0, 0, 0)),
                   pl.BlockSpec((1, 1), lambda h: (0, 0))),
        compiler_params=_params("arbitrary"),
        name="attn_prep",
    )(rel_bias, lq1.reshape(1, -1), lk1.reshape(1, -1), lq2.reshape(1, -1), lk2.reshape(1, -1))


def _diff_attn_kernel(lam_ref, q_ref, k_ref, v_ref, tbl_ref, g_ref, o_ref, *,
                      tq, tk, seq, lambda_init):
    i = pl.program_id(2)
    dh = DIFF_HEAD_DIM
    scale = dh ** -0.5 * LOG2_E
    q = q_ref[0]
    first_half = lax.broadcasted_iota(jnp.int32, q.shape, 1) < dh
    zero = jnp.zeros(q.shape, q.dtype)
    qq = jnp.concatenate([jnp.where(first_half, q, zero), jnp.where(first_half, zero, q)], axis=0)
    tiles_per_chunk = tk // tq
    parts = []
    for c in range(seq // tk):
        keys = slice(c * tk, (c + 1) * tk)
        s = lax.dot_general(qq, k_ref[0, keys, :], (((1,), (1,)), ((), ())),
                            preferred_element_type=F32)
        bias = jnp.concatenate(
            [tbl_ref[0, jnp.clip(c * tiles_per_chunk + j - i + N_BIAS_TILES // 2,
                                 0, N_BIAS_TILES - 1)] for j in range(tiles_per_chunk)], axis=1)
        t = s * scale + jnp.concatenate([bias, bias], axis=0)
        m = jnp.max(t, axis=-1, keepdims=True)
        e = jnp.exp2(t - m)
        parts.append((m, jnp.sum(e, axis=-1, keepdims=True),
                      jnp.dot(e.astype(BF16), v_ref[0, keys, :], preferred_element_type=F32)))
    m_all = parts[0][0]
    for m, _, _ in parts[1:]:
        m_all = jnp.maximum(m_all, m)
    l_all = jnp.zeros_like(m_all)
    o = jnp.zeros(parts[0][2].shape, F32)
    for m, l, oc in parts:
        w = jnp.exp2(m - m_all)
        l_all = l_all + w * l
        o = o + w * oc
    o = o * (1.0 / l_all)
    o = o[:tq] - lam_ref[0, 0] * o[tq:]
    y = o * lax.rsqrt(jnp.mean(o * o, axis=-1, keepdims=True) + EPS)
    o_ref[0] = ((y * g_ref[...]) * (1.0 - lambda_init)).astype(o_ref.dtype)


def _diff_attention(u, tbl, lam, g_subln, *, batch, seq, n_heads, tq, tk, lambda_init):
    hw = 2 * DIFF_HEAD_DIM
    assert tbl.shape[2] == tq and tk % tq == 0 and seq % tk == 0
    return pl.pallas_call(
        functools.partial(_diff_attn_kernel, tq=tq, tk=tk, seq=seq, lambda_init=lambda_init),
        out_shape=jax.ShapeDtypeStruct((batch, seq, n_heads * hw), BF16),
        grid=(batch, n_heads, seq // tq),
        in_specs=[
            pl.BlockSpec(memory_space=pltpu.SMEM),
            pl.BlockSpec((1, tq, hw), lambda b, h, i: (b, i, h)),
            pl.BlockSpec((1, seq, hw), lambda b, h, i: (b, 0, n_heads + h)),
            pl.BlockSpec((1, seq, hw), lambda b, h, i: (b, 0, 2 * n_heads + h)),
            pl.BlockSpec((1, N_BIAS_TILES, tq, tq), lambda b, h, i: (h, 0, 0, 0)),
            pl.BlockSpec((1, hw), lambda b, h, i: (0, 0)),
        ],
        out_specs=pl.BlockSpec((1, tq, hw), lambda b, h, i: (b, i, h)),
        compiler_params=_params("arbitrary", "arbitrary", "arbitrary"),
        name="diff_attention",
    )(lam, u, u, u, tbl, g_subln.reshape(1, hw))


def _conv_kernel(a_ref, ap_ref, an_ref, g_ref, gp_ref, gn_ref, w_ref, b_ref, lg_ref, lb_ref,
                 o_ref, z_ref, zs_ref, y_ref, *, ts):
    i = pl.program_id(1)
    last = pl.num_programs(1) - 1

    def glu(a, g):
        return a[0].astype(F32) * jax.nn.sigmoid(g[0].astype(F32))

    z_ref[:HALO_ROWS, :] = glu(ap_ref, gp_ref) * (i > 0).astype(F32)
    z_ref[HALO_ROWS:HALO_ROWS + ts, :] = glu(a_ref, g_ref)
    z_ref[HALO_ROWS + ts:, :] = glu(an_ref, gn_ref) * (i < last).astype(F32)
    first = HALO_ROWS - CONV_KERNEL // 2
    span = zs_ref.shape[1]
    for res in range(1, 8):
        zs_ref[res - 1] = z_ref[res:res + span, :]
    n_cblocks = y_ref.shape[1] // CONV_CBLOCK
    for r0 in range(0, ts, CONV_RBLOCK):
        def channel_block(cb, carry, r0=r0):
            cols = pl.ds(pl.multiple_of(cb * CONV_CBLOCK, CONV_CBLOCK), CONV_CBLOCK)
            acc = jnp.zeros((CONV_RBLOCK, CONV_CBLOCK), F32)
            for t in range(CONV_KERNEL):
                res, base = (first + t) % 8, (first + t) // 8 * 8 + r0
                rows = slice(base, base + CONV_RBLOCK)
                shifted = z_ref[rows, cols] if res == 0 else zs_ref[res - 1, rows, cols]
                acc = acc + w_ref[t:t + 1, cols] * shifted
            y_ref[r0:r0 + CONV_RBLOCK, cols] = acc + b_ref[:, cols]
            return carry
        lax.fori_loop(0, n_cblocks, channel_block, 0)
    y = y_ref[...]
    yc = y - jnp.mean(y, axis=-1, keepdims=True)
    yn = yc * lax.rsqrt(jnp.mean(yc * yc, axis=-1, keepdims=True) + EPS)
    yn = yn * lg_ref[...] + lb_ref[...]
    o_ref[0] = (yn * jax.nn.sigmoid(yn)).astype(o_ref.dtype)


def _conv_module(u, w_dw, b_dw, g_ln, b_ln, *, batch, seq, col_block, ts):
    cw = w_dw.shape[1]
    hb = ts // HALO_ROWS
    n_hblocks = seq // HALO_ROWS

    def main(c):
        return pl.BlockSpec((1, ts, cw), lambda b, i: (b, i, c))

    def prev(c):
        return pl.BlockSpec((1, HALO_ROWS, cw), lambda b, i: (b, jnp.maximum(i * hb - 1, 0), c))

    def nxt(c):
        return pl.BlockSpec((1, HALO_ROWS, cw),
                            lambda b, i: (b, jnp.minimum((i + 1) * hb, n_hblocks - 1), c))

    row = pl.BlockSpec((1, cw), lambda b, i: (0, 0))
    return pl.pallas_call(
        functools.partial(_conv_kernel, ts=ts),
        out_shape=jax.ShapeDtypeStruct((batch, seq, cw), BF16),
        grid=(batch, seq // ts),
        in_specs=[main(col_block), prev(col_block), nxt(col_block),
                  main(col_block + 1), prev(col_block + 1), nxt(col_block + 1),
                  pl.BlockSpec((CONV_KERNEL, cw), lambda b, i: (0, 0)), row, row, row],
        out_specs=pl.BlockSpec((1, ts, cw), lambda b, i: (b, i, 0)),
        scratch_shapes=[pltpu.VMEM((ts + 2 * HALO_ROWS, cw), F32),
                        pltpu.VMEM((7, ts + 2 * HALO_ROWS - 8, cw), F32),
                        pltpu.VMEM((ts, cw), F32)],
        compiler_params=_params("arbitrary", "arbitrary"),
        name="conv_module",
    )(u, u, u, u, u, u, w_dw, b_dw.reshape(1, cw), g_ln.reshape(1, cw), b_ln.reshape(1, cw))


def _cross_attn_kernel(q_ref, k_ref, v_ref, o_ref, *, head_dim):
    scale = head_dim ** -0.5
    for h in range(N_CROSS_HEADS):
        cols = slice(h * head_dim, (h + 1) * head_dim)
        s = lax.dot_general(q_ref[:, cols], k_ref[:, cols], (((1,), (1,)), ((), ())),
                            preferred_element_type=F32) * scale
        e = jnp.exp(s - jnp.max(s, axis=-1, keepdims=True))
        p = e * (1.0 / jnp.sum(e, axis=-1, keepdims=True))
        o_ref[:, cols] = jnp.dot(p.astype(BF16), v_ref[:, cols],
                                 preferred_element_type=F32).astype(o_ref.dtype)


def _cross_attention(q, k, v, *, batch, seq, n_mem, tm):
    width = q.shape[1]
    qb = seq // tm
    return pl.pallas_call(
        functools.partial(_cross_attn_kernel, head_dim=width // N_CROSS_HEADS),
        out_shape=jax.ShapeDtypeStruct(q.shape, BF16),
        grid=(batch, qb),
        in_specs=[pl.BlockSpec((tm, width), lambda b, i: (b * qb + i, 0)),
                  pl.BlockSpec((n_mem, width), lambda b, i: (b, 0)),
                  pl.BlockSpec((n_mem, width), lambda b, i: (b, 0))],
        out_specs=pl.BlockSpec((tm, width), lambda b, i: (b * qb + i, 0)),
        compiler_params=_params("arbitrary", "arbitrary"),
        name="cross_attention",
    )(q, k, v)


def _pack_bf16_pairs(lo_half, hi_half):
    a = pltpu.bitcast(lo_half, U32)
    b = pltpu.bitcast(hi_half, U32)
    return (a & jnp.uint32(0xFFFF0000)) | (b >> 16)


def _unpack_bf16_pairs(words):
    a = pltpu.bitcast(words & jnp.uint32(0xFFFF0000), F32)
    b = pltpu.bitcast(words << 16, F32)
    return a.astype(BF16), b.astype(BF16)


def _router_kernel(x_ref, g_ref, w_ref, b_ref, hp_ref, idx_ref, gate_ref, rank_ref, cnt_ref,
                   seen_ref):
    @pl.when(pl.program_id(0) == 0)
    def _():
        seen_ref[...] = jnp.zeros(seen_ref.shape, F32)

    x = x_ref[...]
    tm, d = x.shape
    y = x * lax.rsqrt(jnp.mean(x * x, axis=-1, keepdims=True) + EPS)
    hb = (y * g_ref[...]).astype(BF16)
    hp_ref[...] = _pack_bf16_pairs(hb[:, :d // 2].astype(F32), hb[:, d // 2:].astype(F32))
    logits = jnp.dot(hb, w_ref[...].astype(BF16), preferred_element_type=F32) + b_ref[...]
    n_exp = logits.shape[1]
    lane = lax.broadcasted_iota(jnp.int32, logits.shape, 1)
    kl = lax.broadcasted_iota(jnp.int32, idx_ref.shape, 1)
    vals = logits
    top_v = jnp.zeros(gate_ref.shape, F32)
    top_i = jnp.zeros(idx_ref.shape, jnp.int32)
    picks = []
    for kk in range(TOP_K):
        mx = jnp.max(vals, axis=-1, keepdims=True)
        sel = jnp.min(jnp.where(vals == mx, lane, n_exp), axis=-1, keepdims=True)
        top_v = jnp.where(kl == kk, mx, top_v)
        top_i = jnp.where(kl == kk, sel, top_i)
        picks.append(lane == sel)
        vals = jnp.where(picks[-1], -jnp.inf, vals)
    e = jnp.exp(top_v - jnp.max(top_v, axis=-1, keepdims=True))
    gate_ref[...] = e / jnp.sum(e, axis=-1, keepdims=True)
    idx_ref[...] = top_i
    member = jnp.zeros(logits.shape, F32)
    for pick in picks:
        member = member + pick.astype(F32)
    earlier = (lax.broadcasted_iota(jnp.int32, (tm, tm), 0)
               > lax.broadcasted_iota(jnp.int32, (tm, tm), 1)).astype(BF16)
    before = seen_ref[...] + jnp.dot(earlier, member.astype(BF16), preferred_element_type=F32)
    rank = jnp.zeros(rank_ref.shape, F32)
    for kk, pick in enumerate(picks):
        rank = jnp.where(kl == kk, jnp.sum(jnp.where(pick, before, 0.0), axis=-1, keepdims=True),
                         rank)
    rank_ref[...] = rank.astype(jnp.int32)
    seen_ref[...] = seen_ref[...] + jnp.sum(member, axis=0, keepdims=True)
    cnt_ref[...] = seen_ref[...].astype(jnp.int32)


def _router(x, g, w_router, b_router, layer, *, tm):
    n, d = x.shape
    n_exp = w_router.shape[2]
    per_tok = pl.BlockSpec((tm, TOP_K), lambda i: (i, 0))
    return pl.pallas_call(
        _router_kernel,
        out_shape=(jax.ShapeDtypeStruct((n, d // 2), U32),
                   jax.ShapeDtypeStruct((n, TOP_K), jnp.int32),
                   jax.ShapeDtypeStruct((n, TOP_K), F32),
                   jax.ShapeDtypeStruct((n, TOP_K), jnp.int32),
                   jax.ShapeDtypeStruct((1, n_exp), jnp.int32)),
        grid=(n // tm,),
        in_specs=[pl.BlockSpec((tm, d), lambda i: (i, 0)),
                  pl.BlockSpec((1, d), lambda i: (0, 0)),
                  pl.BlockSpec((None, d, n_exp), lambda i: (layer, 0, 0)),
                  pl.BlockSpec((1, n_exp), lambda i: (0, 0))],
        out_specs=(pl.BlockSpec((tm, d // 2), lambda i: (i, 0)), per_tok, per_tok, per_tok,
                   pl.BlockSpec((1, n_exp), lambda i: (0, 0))),
        scratch_shapes=[pltpu.VMEM((1, n_exp), F32)],
        compiler_params=_params("arbitrary"),
        name="router",
    )(x, g.reshape(1, d), w_router, b_router.reshape(1, n_exp))


def _dispatch_kernel(zt_ref, dest_ref, src_ref, dst_ref, zero_ref, zsem, sem, *, tt, tm):
    @pl.when(pl.program_id(0) == 0)
    def _():
        zero_ref[...] = jnp.zeros(zero_ref.shape, zero_ref.dtype)

        def zero_copy(i):
            return pltpu.make_async_copy(zero_ref, dst_ref.at[pl.ds(zt_ref[i] * tm, tm)], zsem.at[0])

        def start(i, carry):
            @pl.when(zt_ref[i] >= 0)
            def _():
                zero_copy(i).start()
            return carry

        def wait(i, carry):
            @pl.when(zt_ref[i] >= 0)
            def _():
                zero_copy(i).wait()
            return carry

        lax.fori_loop(0, zt_ref.shape[0], start, 0)
        lax.fori_loop(0, zt_ref.shape[0], wait, 0)

    def row_copy(r, kk):
        return pltpu.make_async_copy(src_ref.at[pl.ds(r, 1)],
                                     dst_ref.at[pl.ds(dest_ref[0, 0, r * TOP_K + kk], 1)],
                                     sem.at[0])

    def issue(r, carry):
        for kk in range(TOP_K):
            row_copy(r, kk).start(priority=kk % 2)
        return carry

    def drain(r, carry):
        for kk in range(TOP_K):
            row_copy(r, kk).wait()
        return carry

    lax.fori_loop(0, tt, issue, 0, unroll=4)
    lax.fori_loop(0, tt, drain, 0, unroll=4)


def _dispatch(src, dest, zero_tiles, *, p, tt, tm):
    n, half = src.shape
    nt = n // tt
    return pl.pallas_call(
        functools.partial(_dispatch_kernel, tt=tt, tm=tm),
        out_shape=jax.ShapeDtypeStruct((p, half), src.dtype),
        grid_spec=pltpu.PrefetchScalarGridSpec(
            num_scalar_prefetch=1,
            grid=(nt,),
            in_specs=[pl.BlockSpec((1, 1, tt * TOP_K), lambda t, zt: (t, 0, 0),
                                   memory_space=pltpu.SMEM),
                      pl.BlockSpec((tt, half), lambda t, zt: (t, 0))],
            out_specs=pl.BlockSpec(memory_space=pl.ANY),
            scratch_shapes=[pltpu.VMEM((tm, half), src.dtype),
                            pltpu.SemaphoreType.DMA((1,)), pltpu.SemaphoreType.DMA((1,))],
        ),
        compiler_params=_params("arbitrary"),
        name="moe_dispatch",
    )(zero_tiles, dest.reshape(nt, 1, tt * TOP_K), src)


def _is_first_tile_of_expert(te_ref, t):
    return jnp.logical_or(t == 0, te_ref[t] != te_ref[jnp.maximum(t - 1, 0)])


def _gmm1_kernel(te_ref, nu_ref, x_ref, wa_ref, wb_ref, ba_ref, bb_ref, o_ref, wa_bf, wb_bf):
    t = pl.program_id(1)

    @pl.when(t < nu_ref[0])
    def _():
        @pl.when(_is_first_tile_of_expert(te_ref, t))
        def _():
            wa_bf[...] = wa_ref[...].astype(BF16)
            wb_bf[...] = wb_ref[...].astype(BF16)

        x_lo, x_hi = _unpack_bf16_pairs(x_ref[...])
        half = x_lo.shape[1]

        def proj(w_bf, b_ref):
            return (jnp.dot(x_lo, w_bf[:half, :], preferred_element_type=F32)
                    + jnp.dot(x_hi, w_bf[half:, :], preferred_element_type=F32) + b_ref[...])

        a = jnp.minimum(proj(wa_bf, ba_ref), SWIGLU_LIMIT)
        b = jnp.clip(proj(wb_bf, bb_ref), -SWIGLU_LIMIT, SWIGLU_LIMIT)
        o_ref[...] = (a * jax.nn.sigmoid(SWIGLU_ALPHA * a) * (b + 1.0)).astype(o_ref.dtype)


def _gmm1(xg, w1, b1, layer, tile_e, n_used, *, tm, tf):
    p, half = xg.shape
    d = 2 * half
    n_exp, ff = w1.shape[1], w1.shape[3] // 2
    nj = ff // tf
    assert ff % tf == 0 and p % tm == 0

    def row(t, nu):
        return jnp.minimum(t, nu[0] - 1)

    return pl.pallas_call(
        _gmm1_kernel,
        out_shape=jax.ShapeDtypeStruct((p, ff), BF16),
        grid_spec=pltpu.PrefetchScalarGridSpec(
            num_scalar_prefetch=2,
            grid=(nj, p // tm),
            in_specs=[
                pl.BlockSpec((tm, half), lambda j, t, te, nu: (row(t, nu), 0)),
                pl.BlockSpec((None, None, d, tf), lambda j, t, te, nu: (layer, te[t], 0, j)),
                pl.BlockSpec((None, None, d, tf), lambda j, t, te, nu: (layer, te[t], 0, nj + j)),
                pl.BlockSpec((None, None, 1, tf), lambda j, t, te, nu: (layer, te[t], 0, j)),
                pl.BlockSpec((None, None, 1, tf), lambda j, t, te, nu: (layer, te[t], 0, nj + j)),
            ],
            out_specs=pl.BlockSpec((tm, tf), lambda j, t, te, nu: (row(t, nu), j)),
            scratch_shapes=[pltpu.VMEM((d, tf), BF16), pltpu.VMEM((d, tf), BF16)],
        ),
        compiler_params=_params("arbitrary", "arbitrary"),
        name="moe_gmm1",
    )(tile_e, n_used, xg, w1, w1, b1.reshape(b1.shape[0], n_exp, 1, 2 * ff),
      b1.reshape(b1.shape[0], n_exp, 1, 2 * ff))


def _gmm2_kernel(te_ref, nu_ref, x_ref, w_ref, b_ref, o_ref, w_bf):
    t = pl.program_id(1)

    @pl.when(t < nu_ref[0])
    def _():
        @pl.when(_is_first_tile_of_expert(te_ref, t))
        def _():
            w_bf[...] = w_ref[...].astype(BF16)

        o_ref[...] = jnp.dot(x_ref[...], w_bf[...], preferred_element_type=F32) + b_ref[...]


def _gmm2(act, w2, b2, layer, tile_e, n_used, *, tm, tn):
    p, ff = act.shape
    n_exp, d = w2.shape[1], w2.shape[3]
    assert d % tn == 0

    def row(t, nu):
        return jnp.minimum(t, nu[0] - 1)

    return pl.pallas_call(
        _gmm2_kernel,
        out_shape=jax.ShapeDtypeStruct((p, d), F32),
        grid_spec=pltpu.PrefetchScalarGridSpec(
            num_scalar_prefetch=2,
            grid=(d // tn, p // tm),
            in_specs=[
                pl.BlockSpec((tm, ff), lambda j, t, te, nu: (row(t, nu), 0)),
                pl.BlockSpec((None, None, ff, tn), lambda j, t, te, nu: (layer, te[t], 0, j)),
                pl.BlockSpec((None, None, 1, tn), lambda j, t, te, nu: (layer, te[t], 0, j)),
            ],
            out_specs=pl.BlockSpec((tm, tn), lambda j, t, te, nu: (row(t, nu), j)),
            scratch_shapes=[pltpu.VMEM((ff, tn), BF16)],
        ),
        compiler_params=_params("arbitrary", "arbitrary"),
        name="moe_gmm2",
    )(tile_e, n_used, act, w2, b2.reshape(b2.shape[0], n_exp, 1, d))


def _combine_kernel(dcur_ref, dnext_ref, x_ref, gate_ref, g_ref, y_ref, o_ref, buf, sem, *,
                    tt, final_norm):
    t = pl.program_id(0)
    slot = lax.rem(t, 2)

    def row_copy(d_ref, r, kk, s):
        return pltpu.make_async_copy(y_ref.at[pl.ds(d_ref[0, 0, r * TOP_K + kk], 1)],
                                     buf.at[s, kk, pl.ds(r, 1)], sem.at[s])

    def issue(d_ref, s):
        def body(r, carry):
            for kk in range(TOP_K):
                row_copy(d_ref, r, kk, s).start(priority=kk % 2)
            return carry
        lax.fori_loop(0, tt, body, 0, unroll=4)

    @pl.when(t == 0)
    def _():
        issue(dcur_ref, 0)

    @pl.when(t + 1 < pl.num_programs(0))
    def _():
        issue(dnext_ref, 1 - slot)

    def drain(r, carry):
        for kk in range(TOP_K):
            row_copy(dcur_ref, r, kk, slot).wait()
        return carry
    lax.fori_loop(0, tt, drain, 0, unroll=4)

    gate = gate_ref[...]
    acc = x_ref[...]
    for kk in range(TOP_K):
        acc = acc + gate[:, kk:kk + 1] * buf[slot, kk]
    if final_norm:
        acc = acc * lax.rsqrt(jnp.mean(acc * acc, axis=-1, keepdims=True) + EPS) * g_ref[...]
    o_ref[...] = acc


def _combine(x, y, dest, gate, g_final, *, tt, final_norm):
    n, d = x.shape
    nt = n // tt
    dest3 = dest.reshape(nt, 1, tt * TOP_K)
    return pl.pallas_call(
        functools.partial(_combine_kernel, tt=tt, final_norm=final_norm),
        out_shape=jax.ShapeDtypeStruct((n, d), F32),
        grid=(nt,),
        in_specs=[
            pl.BlockSpec((1, 1, tt * TOP_K), lambda t: (t, 0, 0), memory_space=pltpu.SMEM),
            pl.BlockSpec((1, 1, tt * TOP_K), lambda t: (jnp.minimum(t + 1, nt - 1), 0, 0),
                         memory_space=pltpu.SMEM),
            pl.BlockSpec((tt, d), lambda t: (t, 0)),
            pl.BlockSpec((tt, TOP_K), lambda t: (t, 0)),
            pl.BlockSpec((1, d), lambda t: (0, 0)),
            pl.BlockSpec(memory_space=pl.ANY),
        ],
        out_specs=pl.BlockSpec((tt, d), lambda t: (t, 0)),
        scratch_shapes=[pltpu.VMEM((2, TOP_K, tt, d), F32), pltpu.SemaphoreType.DMA((2,))],
        compiler_params=_params("arbitrary"),
        name="moe_combine",
    )(dest3, dest3, x, gate, g_final.reshape(1, d), y)


def _routing_tables(top_idx, rank, counts, tm):
    n = top_idx.shape[0]
    n_exp = counts.shape[0]
    n_tiles = (n * TOP_K) // tm + n_exp
    padded = (counts + tm - 1) // tm * tm
    pad_end = jnp.cumsum(padded)
    pad_start = pad_end - padded
    experts = jnp.arange(n_exp, dtype=jnp.int32)
    dest = rank + jnp.sum(jnp.where(top_idx[..., None] == experts, pad_start, 0), axis=-1)
    tile_start = jnp.arange(n_tiles, dtype=jnp.int32) * tm
    tile_e = jnp.minimum(jnp.sum(tile_start[:, None] >= pad_end[None, :], axis=-1),
                         n_exp - 1).astype(jnp.int32)
    n_used = pad_end[-1:] // tm
    ragged = jnp.where(counts % tm != 0, pad_end // tm - 1, -1)
    tail = n_used + experts
    zero_tiles = jnp.concatenate([ragged, jnp.where(tail < n_tiles, tail, -1)])
    return (dest.astype(jnp.int32), tile_e, n_used.astype(jnp.int32),
            zero_tiles.astype(jnp.int32), n_tiles * tm)


def _pick(n, pref):
    t = min(n, pref)
    while n % t or t % 8:
        t -= 1
    return t


def kernel(x, mem, g_mix, w_in, lambda_q1, lambda_k1, lambda_q2, lambda_k2, g_subln, w_dw, b_dw,
           g_conv_ln, b_conv_ln, w_out, rel_bias, g_cross, g_mem, w_cq, w_ck, w_cv, w_co, g_ffn,
           w_router, b_router, w1, b1, w2, b2, g_final):
    batch, seq, d = x.shape
    n = batch * seq
    n_mem = mem.shape[1]
    depth = g_mix.shape[0]
    n_heads = rel_bias.shape[1]
    attn_w = n_heads * 2 * DIFF_HEAD_DIM
    conv_w = w_dw.shape[2]
    in_w = w_in.shape[2]
    n_exp = w_router.shape[2]
    assert in_w == 3 * attn_w + 2 * conv_w and (3 * attn_w) % conv_w == 0

    tq = _pick(seq, 512)
    xf = x.reshape(n, d)
    memf = mem.reshape(batch * n_mem, d)
    for l in range(depth):
        lambda_init = 0.8 - 0.6 * math.exp(-0.3 * l)
        tbl, lam = _attn_prep(rel_bias, lambda_q1[l], lambda_k1[l], lambda_q2[l], lambda_k2[l],
                              tb=tq, lambda_init=lambda_init)
        h = _rmsnorm(xf, g_mix[l], rows=_pick(n, 256))
        u = _dense([h], w_in, l, tm=_pick(n, 512), tn=_pick(in_w, 1024), out_dtype=BF16,
                   name="in_proj").reshape(batch, seq, in_w)
        attn = _diff_attention(u, tbl, lam, g_subln[l], batch=batch, seq=seq, n_heads=n_heads,
                               tq=tq, tk=_pick(seq, 1024), lambda_init=lambda_init)
        conv = _conv_module(u, w_dw[l], b_dw[l], g_conv_ln[l], b_conv_ln[l], batch=batch, seq=seq,
                            col_block=3 * attn_w // conv_w, ts=_pick(seq, 256))
        xf = _dense([attn.reshape(n, attn_w), conv.reshape(n, conv_w)], w_out, l,
                    tm=_pick(n, 512), tn=_pick(d, 512), out_dtype=F32, res=xf, name="out_proj")
        h = _rmsnorm(xf, g_cross[l], rows=_pick(n, 256))
        m = _rmsnorm(memf, g_mem[l], rows=_pick(batch * n_mem, 256))
        cross_w = w_cq.shape[2]
        qc = _dense([h], w_cq, l, tm=_pick(n, 512), tn=_pick(cross_w, 512), out_dtype=BF16,
                    name="cross_q")
        kc = _dense([m], w_ck, l, tm=_pick(batch * n_mem, 512), tn=_pick(cross_w, 512),
                    out_dtype=BF16, name="cross_k")
        vc = _dense([m], w_cv, l, tm=_pick(batch * n_mem, 512), tn=_pick(cross_w, 512),
                    out_dtype=BF16, name="cross_v")
        oc = _cross_attention(qc, kc, vc, batch=batch, seq=seq, n_mem=n_mem, tm=_pick(seq, 512))
        xf = _dense([oc], w_co, l, tm=_pick(n, 512), tn=_pick(d, 2048), out_dtype=F32, res=xf,
                    name="cross_o")
        tm = 512
        hp, top_idx, gate, rank, counts = _router(xf, g_ffn[l], w_router, b_router[l], l,
                                                  tm=_pick(n, 256))
        dest, tile_e, n_used, zero_tiles, p_rows = _routing_tables(top_idx, rank, counts[0], tm)
        xg = _dispatch(hp, dest, zero_tiles, p=p_rows, tt=_pick(n, 256), tm=tm)
        ff = w2.shape[2]
        act = _gmm1(xg, w1, b1, l, tile_e, n_used, tm=tm, tf=_pick(ff, 512))
        y = _gmm2(act, w2, b2, l, tile_e, n_used, tm=tm, tn=_pick(d, 2048))
        xf = _combine(xf, y, dest, gate, g_final, tt=_pick(n, 64), final_norm=l == depth - 1)
    return xf.reshape(batch, seq, d)
```

```python
import functools
import math

import jax
import jax.numpy as jnp
from jax import lax
from jax.experimental import pallas as pl
from jax.experimental.pallas import tpu as pltpu

F32 = jnp.float32
BF16 = jnp.bfloat16
U32 = jnp.uint32

DIFF_HEAD_DIM = 128
CONV_KERNEL = 31
N_BUCKETS = 32
MAX_DISTANCE = 128
N_CROSS_HEADS = 4
TOP_K = 4
SWIGLU_LIMIT = 7.0
SWIGLU_ALPHA = 1.702
EPS = 1e-6
LOG2_E = math.log2(math.e)
N_BIAS_TILES = 5

V7X_VMEM_BYTES = 64 * 1024 * 1024
VMEM_LIMIT_BYTES = V7X_VMEM_BYTES - 6 * 1024 * 1024
LANES = 128
MOE_SUB_ROWS = 128
CONV_RBLOCK = 64
CONV_CBLOCK = 256
HALO_ROWS = 16

_T5_LOG_THRESHOLDS = tuple(
    math.ceil(8 * (MAX_DISTANCE / 8) ** (k / 8) - 1e-9) for k in range(1, 8))


def _params(*semantics):
    return pltpu.CompilerParams(dimension_semantics=semantics,
                                vmem_limit_bytes=VMEM_LIMIT_BYTES)


def _rmsnorm_kernel(x_ref, g_ref, o_ref):
    x = x_ref[...]
    y = x * lax.rsqrt(jnp.mean(x * x, axis=-1, keepdims=True) + EPS)
    o_ref[...] = (y * g_ref[...]).astype(o_ref.dtype)


def _rmsnorm(x, g, *, rows):
    n, d = x.shape
    return pl.pallas_call(
        _rmsnorm_kernel,
        out_shape=jax.ShapeDtypeStruct((n, d), BF16),
        grid=(n // rows,),
        in_specs=[pl.BlockSpec((rows, d), lambda i: (i, 0)),
                  pl.BlockSpec((1, d), lambda i: (0, 0))],
        out_specs=pl.BlockSpec((rows, d), lambda i: (i, 0)),
        compiler_params=_params("arbitrary"),
        name="rmsnorm",
    )(x, g.reshape(1, d))


def _dense_kernel(*refs, k_splits, has_norm, has_res):
    n_lhs = len(k_splits)
    x_refs = refs[:n_lhs]
    rest = list(refs[n_lhs:-2])
    g_ref = rest.pop(0) if has_norm else None
    w_ref = rest.pop(0)
    res_ref = rest.pop(0) if has_res else None
    o_ref, wbf_ref = refs[-2], refs[-1]

    @pl.when(pl.program_id(1) == 0)
    def _():
        wbf_ref[...] = w_ref[...].astype(BF16)

    acc = None
    k0 = 0
    for x_ref, kw in zip(x_refs, k_splits):
        x = x_ref[...]
        if has_norm:
            x = x * lax.rsqrt(jnp.mean(x * x, axis=-1, keepdims=True) + EPS)
            x = (x * g_ref[...]).astype(BF16)
        part = jnp.dot(x, wbf_ref[k0:k0 + kw, :], preferred_element_type=F32)
        acc = part if acc is None else acc + part
        k0 += kw
    if has_res:
        acc = res_ref[...] + acc
    o_ref[...] = acc.astype(o_ref.dtype)


def _dense(xs, w, layer, *, tm, tn, out_dtype, norm_g=None, res=None, name):
    m = xs[0].shape[0]
    k_splits = tuple(x.shape[1] for x in xs)
    k, n = w.shape[1], w.shape[2]
    assert sum(k_splits) == k and m % tm == 0 and n % tn == 0
    assert norm_g is None or len(xs) == 1
    in_specs = [pl.BlockSpec((tm, kw), lambda j, i: (i, 0)) for kw in k_splits]
    args = list(xs)
    if norm_g is not None:
        in_specs.append(pl.BlockSpec((1, k), lambda j, i: (0, 0)))
        args.append(norm_g.reshape(1, k))
    in_specs.append(pl.BlockSpec((None, k, tn), lambda j, i: (layer, 0, j)))
    args.append(w)
    if res is not None:
        in_specs.append(pl.BlockSpec((tm, tn), lambda j, i: (i, j)))
        args.append(res)
    return pl.pallas_call(
        functools.partial(_dense_kernel, k_splits=k_splits, has_norm=norm_g is not None,
                          has_res=res is not None),
        out_shape=jax.ShapeDtypeStruct((m, n), out_dtype),
        grid=(n // tn, m // tm),
        in_specs=in_specs,
        out_specs=pl.BlockSpec((tm, tn), lambda j, i: (i, j)),
        scratch_shapes=[pltpu.VMEM((k, tn), BF16)],
        compiler_params=_params("arbitrary", "arbitrary"),
        name=name,
    )(*args)


def _t5_bucket(rel):
    half = N_BUCKETS // 2
    max_exact = half // 2
    n = jnp.abs(rel)
    large = jnp.full(rel.shape, max_exact, jnp.int32)
    for thr in _T5_LOG_THRESHOLDS:
        large = large + (n >= thr).astype(jnp.int32)
    return jnp.where(rel > 0, half, 0) + jnp.where(n < max_exact, n, large)


def _t5_bucket_static(rel):
    half = N_BUCKETS // 2
    n = abs(rel)
    large = half // 2 + sum(n >= thr for thr in _T5_LOG_THRESHOLDS)
    return (half if rel > 0 else 0) + (n if n < half // 2 else large)


def _attn_prep_kernel(rb_ref, lq1_ref, lk1_ref, lq2_ref, lk2_ref, tbl_ref, lam_ref, *,
                      tb, lambda_init):
    h = pl.program_id(0)
    a = lax.broadcasted_iota(jnp.int32, (tb, tb), 0)
    b = lax.broadcasted_iota(jnp.int32, (tb, tb), 1)
    for u in range(N_BIAS_TILES):
        origin = (u - N_BIAS_TILES // 2) * tb
        reachable = sorted({_t5_bucket_static(rel)
                            for rel in range(origin - tb + 1, origin + tb)})
        bucket = _t5_bucket(origin + b - a)
        val = jnp.full((tb, tb), rb_ref[reachable[0], h], F32)
        for bk in reachable[1:]:
            val = jnp.where(bucket == bk, rb_ref[bk, h], val)
        tbl_ref[0, u] = val * LOG2_E
    s1 = jnp.sum(lq1_ref[...] * lk1_ref[...], axis=-1, keepdims=True)
    s2 = jnp.sum(lq2_ref[...] * lk2_ref[...], axis=-1, keepdims=True)
    lam_ref[...] = jnp.exp(s1) - jnp.exp(s2) + lambda_init


def _attn_prep(rel_bias, lq1, lk1, lq2, lk2, *, tb, lambda_init):
    n_heads = rel_bias.shape[1]
    assert tb > MAX_DISTANCE
    vec = pl.BlockSpec((1, DIFF_HEAD_DIM), lambda h: (0, 0))
    return pl.pallas_call(
        functools.partial(_attn_prep_kernel, tb=tb, lambda_init=lambda_init),
        out_shape=(jax.ShapeDtypeStruct((n_heads, N_BIAS_TILES, tb, tb), F32),
                   jax.ShapeDtypeStruct((1, 1), F32)),
        grid=(n_heads,),
        in_specs=[pl.BlockSpec(memory_space=pltpu.SMEM), vec, vec, vec, vec],
        out_specs=(pl.BlockSpec((1, N_BIAS_TILES, tb, tb), lambda h: (h, 0, 0, 0)),
                   pl.BlockSpec((1, 1), lambda h: (0, 0))),
        compiler_params=_params("arbitrary"),
        name="attn_prep",
    )(rel_bias, lq1.reshape(1, -1), lk1.reshape(1, -1), lq2.reshape(1, -1), lk2.reshape(1, -1))


def _diff_attn_kernel(lam_ref, q_ref, k_ref, v_ref, tbl_ref, g_ref, o_ref, *,
                      tq, tk, seq, lambda_init):
    i = pl.program_id(2)
    dh = DIFF_HEAD_DIM
    scale = dh ** -0.5 * LOG2_E
    q = q_ref[0]
    first_half = lax.broadcasted_iota(jnp.int32, q.shape, 1) < dh
    zero = jnp.zeros(q.shape, q.dtype)
    qq = jnp.concatenate([jnp.where(first_half, q, zero), jnp.where(first_half, zero, q)], axis=0)
    tiles_per_chunk = tk // tq
    parts = []
    for c in range(seq // tk):
        keys = slice(c * tk, (c + 1) * tk)
        s = lax.dot_general(qq, k_ref[0, keys, :], (((1,), (1,)), ((), ())),
                            preferred_element_type=F32)
        bias = jnp.concatenate(
            [tbl_ref[0, jnp.clip(c * tiles_per_chunk + j - i + N_BIAS_TILES // 2,
                                 0, N_BIAS_TILES - 1)] for j in range(tiles_per_chunk)], axis=1)
        t = s * scale + jnp.concatenate([bias, bias], axis=0)
        m = jnp.max(t, axis=-1, keepdims=True)
        e = jnp.exp2(t - m)
        parts.append((m, jnp.sum(e, axis=-1, keepdims=True),
                      jnp.dot(e.astype(BF16), v_ref[0, keys, :], preferred_element_type=F32)))
    m_all = parts[0][0]
    for m, _, _ in parts[1:]:
        m_all = jnp.maximum(m_all, m)
    l_all = jnp.zeros_like(m_all)
    o = jnp.zeros(parts[0][2].shape, F32)
    for m, l, oc in parts:
        w = jnp.exp2(m - m_all)
        l_all = l_all + w * l
        o = o + w * oc
    o = o * (1.0 / l_all)
    o = o[:tq] - lam_ref[0, 0] * o[tq:]
    y = o * lax.rsqrt(jnp.mean(o * o, axis=-1, keepdims=True) + EPS)
    o_ref[0] = ((y * g_ref[...]) * (1.0 - lambda_init)).astype(o_ref.dtype)


def _diff_attention(u, tbl, lam, g_subln, *, batch, seq, n_heads, tq, tk, lambda_init):
    hw = 2 * DIFF_HEAD_DIM
    assert tbl.shape[2] == tq and tk % tq == 0 and seq % tk == 0
    return pl.pallas_call(
        functools.partial(_diff_attn_kernel, tq=tq, tk=tk, seq=seq, lambda_init=lambda_init),
        out_shape=jax.ShapeDtypeStruct((batch, seq, n_heads * hw), BF16),
        grid=(batch, n_heads, seq // tq),
        in_specs=[
            pl.BlockSpec(memory_space=pltpu.SMEM),
            pl.BlockSpec((1, tq, hw), lambda b, h, i: (b, i, h)),
            pl.BlockSpec((1, seq, hw), lambda b, h, i: (b, 0, n_heads + h)),
            pl.BlockSpec((1, seq, hw), lambda b, h, i: (b, 0, 2 * n_heads + h)),
            pl.BlockSpec((1, N_BIAS_TILES, tq, tq), lambda b, h, i: (h, 0, 0, 0)),
            pl.BlockSpec((1, hw), lambda b, h, i: (0, 0)),
        ],
        out_specs=pl.BlockSpec((1, tq, hw), lambda b, h, i: (b, i, h)),
        compiler_params=_params("arbitrary", "arbitrary", "arbitrary"),
        name="diff_attention",
    )(lam, u, u, u, tbl, g_subln.reshape(1, hw))


def _conv_kernel(a_ref, ap_ref, an_ref, g_ref, gp_ref, gn_ref, w_ref, b_ref, lg_ref, lb_ref,
                 o_ref, z_ref, zs_ref, y_ref, *, ts):
    i = pl.program_id(1)
    last = pl.num_programs(1) - 1

    def glu(a, g):
        return a[0].astype(F32) * jax.nn.sigmoid(g[0].astype(F32))

    z_ref[:HALO_ROWS, :] = glu(ap_ref, gp_ref) * (i > 0).astype(F32)
    z_ref[HALO_ROWS:HALO_ROWS + ts, :] = glu(a_ref, g_ref)
    z_ref[HALO_ROWS + ts:, :] = glu(an_ref, gn_ref) * (i < last).astype(F32)
    first = HALO_ROWS - CONV_KERNEL // 2
    span = zs_ref.shape[1]
    for res in range(1, 8):
        zs_ref[res - 1] = z_ref[res:res + span, :]
    n_cblocks = y_ref.shape[1] // CONV_CBLOCK
    for r0 in range(0, ts, CONV_RBLOCK):
        def channel_block(cb, carry, r0=r0):
            cols = pl.ds(pl.multiple_of(cb * CONV_CBLOCK, CONV_CBLOCK), CONV_CBLOCK)
            acc = jnp.zeros((CONV_RBLOCK, CONV_CBLOCK), F32)
            for t in range(CONV_KERNEL):
                res, base = (first + t) % 8, (first + t) // 8 * 8 + r0
                rows = slice(base, base + CONV_RBLOCK)
                shifted = z_ref[rows, cols] if res == 0 else zs_ref[res - 1, rows, cols]
                acc = acc + w_ref[t:t + 1, cols] * shifted
            y_ref[r0:r0 + CONV_RBLOCK, cols] = acc + b_ref[:, cols]
            return carry
        lax.fori_loop(0, n_cblocks, channel_block, 0)
    y = y_ref[...]
    yc = y - jnp.mean(y, axis=-1, keepdims=True)
    yn = yc * lax.rsqrt(jnp.mean(yc * yc, axis=-1, keepdims=True) + EPS)
    yn = yn * lg_ref[...] + lb_ref[...]
    o_ref[0] = (yn * jax.nn.sigmoid(yn)).astype(o_ref.dtype)


def _conv_module(u, w_dw, b_dw, g_ln, b_ln, *, batch, seq, col_block, ts):
    cw = w_dw.shape[1]
    hb = ts // HALO_ROWS
    n_hblocks = seq // HALO_ROWS

    def main(c):
        return pl.BlockSpec((1, ts, cw), lambda b, i: (b, i, c))

    def prev(c):
        return pl.BlockSpec((1, HALO_ROWS, cw), lambda b, i: (b, jnp.maximum(i * hb - 1, 0), c))

    def nxt(c):
        return pl.BlockSpec((1, HALO_ROWS, cw),
                            lambda b, i: (b, jnp.minimum((i + 1) * hb, n_hblocks - 1), c))

    row = pl.BlockSpec((1, cw), lambda b, i: (0, 0))
    return pl.pallas_call(
        functools.partial(_conv_kernel, ts=ts),
        out_shape=jax.ShapeDtypeStruct((batch, seq, cw), BF16),
        grid=(batch, seq // ts),
        in_specs=[main(col_block), prev(col_block), nxt(col_block),
                  main(col_block + 1), prev(col_block + 1), nxt(col_block + 1),
                  pl.BlockSpec((CONV_KERNEL, cw), lambda b, i: (0, 0)), row, row, row],
        out_specs=pl.BlockSpec((1, ts, cw), lambda b, i: (b, i, 0)),
        scratch_shapes=[pltpu.VMEM((ts + 2 * HALO_ROWS, cw), F32),
                        pltpu.VMEM((7, ts + 2 * HALO_ROWS - 8, cw), F32),
                        pltpu.VMEM((ts, cw), F32)],
        compiler_params=_params("arbitrary", "arbitrary"),
        name="conv_module",
    )(u, u, u, u, u, u, w_dw, b_dw.reshape(1, cw), g_ln.reshape(1, cw), b_ln.reshape(1, cw))


def _cross_attn_kernel(q_ref, k_ref, v_ref, o_ref, *, head_dim):
    scale = head_dim ** -0.5
    for h in range(N_CROSS_HEADS):
        cols = slice(h * head_dim, (h + 1) * head_dim)
        s = lax.dot_general(q_ref[:, cols], k_ref[:, cols], (((1,), (1,)), ((), ())),
                            preferred_element_type=F32) * scale
        e = jnp.exp(s - jnp.max(s, axis=-1, keepdims=True))
        p = e * (1.0 / jnp.sum(e, axis=-1, keepdims=True))
        o_ref[:, cols] = jnp.dot(p.astype(BF16), v_ref[:, cols],
                                 preferred_element_type=F32).astype(o_ref.dtype)


def _cross_attention(q, k, v, *, batch, seq, n_mem, tm):
    width = q.shape[1]
    qb = seq // tm
    return pl.pallas_call(
        functools.partial(_cross_attn_kernel, head_dim=width // N_CROSS_HEADS),
        out_shape=jax.ShapeDtypeStruct(q.shape, BF16),
        grid=(batch, qb),
        in_specs=[pl.BlockSpec((tm, width), lambda b, i: (b * qb + i, 0)),
                  pl.BlockSpec((n_mem, width), lambda b, i: (b, 0)),
                  pl.BlockSpec((n_mem, width), lambda b, i: (b, 0))],
        out_specs=pl.BlockSpec((tm, width), lambda b, i: (b * qb + i, 0)),
        compiler_params=_params("arbitrary", "arbitrary"),
        name="cross_attention",
    )(q, k, v)


def _pack_bf16_pairs(lo_half, hi_half):
    a = pltpu.bitcast(lo_half, U32)
    b = pltpu.bitcast(hi_half, U32)
    return (a & jnp.uint32(0xFFFF0000)) | (b >> 16)


def _unpack_bf16_pairs(words):
    a = pltpu.bitcast(words & jnp.uint32(0xFFFF0000), F32)
    b = pltpu.bitcast(words << 16, F32)
    return a.astype(BF16), b.astype(BF16)


def _router_kernel(x_ref, g_ref, w_ref, b_ref, hp_ref, idx_ref, gate_ref, rank_ref, cnt_ref,
                   seen_ref):
    @pl.when(pl.program_id(0) == 0)
    def _():
        seen_ref[...] = jnp.zeros(seen_ref.shape, F32)

    x = x_ref[...]
    tm, d = x.shape
    y = x * lax.rsqrt(jnp.mean(x * x, axis=-1, keepdims=True) + EPS)
    hb = (y * g_ref[...]).astype(BF16)
    hp_ref[...] = _pack_bf16_pairs(hb[:, :d // 2].astype(F32), hb[:, d // 2:].astype(F32))
    logits = jnp.dot(hb, w_ref[...].astype(BF16), preferred_element_type=F32) + b_ref[...]
    n_exp = logits.shape[1]
    lane = lax.broadcasted_iota(jnp.int32, logits.shape, 1)
    kl = lax.broadcasted_iota(jnp.int32, idx_ref.shape, 1)
    vals = logits
    top_v = jnp.zeros(gate_ref.shape, F32)
    top_i = jnp.zeros(idx_ref.shape, jnp.int32)
    picks = []
    for kk in range(TOP_K):
        mx = jnp.max(vals, axis=-1, keepdims=True)
        sel = jnp.min(jnp.where(vals == mx, lane, n_exp), axis=-1, keepdims=True)
        top_v = jnp.where(kl == kk, mx, top_v)
        top_i = jnp.where(kl == kk, sel, top_i)
        picks.append(lane == sel)
        vals = jnp.where(picks[-1], -jnp.inf, vals)
    e = jnp.exp(top_v - jnp.max(top_v, axis=-1, keepdims=True))
    gate_ref[...] = e / jnp.sum(e, axis=-1, keepdims=True)
    idx_ref[...] = top_i
    member = jnp.zeros(logits.shape, F32)
    for pick in picks:
        member = member + pick.astype(F32)
    earlier = (lax.broadcasted_iota(jnp.int32, (tm, tm), 0)
               > lax.broadcasted_iota(jnp.int32, (tm, tm), 1)).astype(BF16)
    before = seen_ref[...] + jnp.dot(earlier, member.astype(BF16), preferred_element_type=F32)
    rank = jnp.zeros(rank_ref.shape, F32)
    for kk, pick in enumerate(picks):
        rank = jnp.where(kl == kk, jnp.sum(jnp.where(pick, before, 0.0), axis=-1, keepdims=True),
                         rank)
    rank_ref[...] = rank.astype(jnp.int32)
    seen_ref[...] = seen_ref[...] + jnp.sum(member, axis=0, keepdims=True)
    cnt_ref[...] = seen_ref[...].astype(jnp.int32)


def _router(x, g, w_router, b_router, layer, *, tm):
    n, d = x.shape
    n_exp = w_router.shape[2]
    per_tok = pl.BlockSpec((tm, TOP_K), lambda i: (i, 0))
    return pl.pallas_call(
        _router_kernel,
        out_shape=(jax.ShapeDtypeStruct((n, d // 2), U32),
                   jax.ShapeDtypeStruct((n, TOP_K), jnp.int32),
                   jax.ShapeDtypeStruct((n, TOP_K), F32),
                   jax.ShapeDtypeStruct((n, TOP_K), jnp.int32),
                   jax.ShapeDtypeStruct((1, n_exp), jnp.int32)),
        grid=(n // tm,),
        in_specs=[pl.BlockSpec((tm, d), lambda i: (i, 0)),
                  pl.BlockSpec((1, d), lambda i: (0, 0)),
                  pl.BlockSpec((None, d, n_exp), lambda i: (layer, 0, 0)),
                  pl.BlockSpec((1, n_exp), lambda i: (0, 0))],
        out_specs=(pl.BlockSpec((tm, d // 2), lambda i: (i, 0)), per_tok, per_tok, per_tok,
                   pl.BlockSpec((1, n_exp), lambda i: (0, 0))),
        scratch_shapes=[pltpu.VMEM((1, n_exp), F32)],
        compiler_params=_params("arbitrary"),
        name="router",
    )(x, g.reshape(1, d), w_router, b_router.reshape(1, n_exp))


def _dispatch_kernel(zt_ref, dest_ref, src_ref, dst_ref, zero_ref, zsem, sem, *, tt, tm):
    @pl.when(pl.program_id(0) == 0)
    def _():
        zero_ref[...] = jnp.zeros(zero_ref.shape, zero_ref.dtype)

        def zero_copy(i):
            return pltpu.make_async_copy(zero_ref, dst_ref.at[pl.ds(zt_ref[i] * tm, tm)], zsem.at[0])

        def start(i, carry):
            @pl.when(zt_ref[i] >= 0)
            def _():
                zero_copy(i).start()
            return carry

        def wait(i, carry):
            @pl.when(zt_ref[i] >= 0)
            def _():
                zero_copy(i).wait()
            return carry

        lax.fori_loop(0, zt_ref.shape[0], start, 0)
        lax.fori_loop(0, zt_ref.shape[0], wait, 0)

    def row_copy(r, kk):
        return pltpu.make_async_copy(src_ref.at[pl.ds(r, 1)],
                                     dst_ref.at[pl.ds(dest_ref[0, 0, r * TOP_K + kk], 1)],
                                     sem.at[0])

    for r in range(tt):
        for kk in range(TOP_K):
            row_copy(r, kk).start(priority=kk % 2)
    for kk in range(TOP_K):
        pltpu.make_async_copy(src_ref, dst_ref.at[pl.ds(0, tt)], sem.at[0]).wait()


def _dispatch(src, dest, zero_tiles, *, p, tt, tm):
    n, half = src.shape
    nt = n // tt
    return pl.pallas_call(
        functools.partial(_dispatch_kernel, tt=tt, tm=tm),
        out_shape=jax.ShapeDtypeStruct((p, half), src.dtype),
        grid_spec=pltpu.PrefetchScalarGridSpec(
            num_scalar_prefetch=1,
            grid=(nt,),
            in_specs=[pl.BlockSpec((1, 1, tt * TOP_K), lambda t, zt: (t, 0, 0),
                                   memory_space=pltpu.SMEM),
                      pl.BlockSpec((tt, half), lambda t, zt: (t, 0))],
            out_specs=pl.BlockSpec(memory_space=pl.ANY),
            scratch_shapes=[pltpu.VMEM((tm, half), src.dtype),
                            pltpu.SemaphoreType.DMA((1,)), pltpu.SemaphoreType.DMA((1,))],
        ),
        compiler_params=_params("arbitrary"),
        name="moe_dispatch",
    )(zero_tiles, dest.reshape(nt, 1, tt * TOP_K), src)


def _is_first_tile_of_expert(te_ref, t):
    return jnp.logical_or(t == 0, te_ref[t] != te_ref[jnp.maximum(t - 1, 0)])


def _on_valid_rows(valid, o_ref, compute):
    tm = o_ref.shape[0]

    @pl.when(valid == tm)
    def _():
        compute(slice(0, tm))

    @pl.when(valid < tm)
    def _():
        for r0 in range(0, tm, MOE_SUB_ROWS):
            rows = slice(r0, r0 + MOE_SUB_ROWS)

            @pl.when(r0 < valid)
            def _(rows=rows):
                compute(rows)

            @pl.when(r0 >= valid)
            def _(rows=rows):
                o_ref[rows, :] = jnp.zeros((MOE_SUB_ROWS, o_ref.shape[1]), o_ref.dtype)


def _gmm1_kernel(te_ref, nu_ref, tr_ref, x_ref, wa_ref, wb_ref, ba_ref, bb_ref, o_ref,
                 wa_bf, wb_bf):
    t = pl.program_id(1)

    @pl.when(t < nu_ref[0])
    def _():
        @pl.when(_is_first_tile_of_expert(te_ref, t))
        def _():
            wa_bf[...] = wa_ref[...].astype(BF16)
            wb_bf[...] = wb_ref[...].astype(BF16)

        def compute(rows):
            x_lo, x_hi = _unpack_bf16_pairs(x_ref[rows, :])
            half = x_lo.shape[1]

            def proj(w_bf, b_ref):
                return (jnp.dot(x_lo, w_bf[:half, :], preferred_element_type=F32)
                        + jnp.dot(x_hi, w_bf[half:, :], preferred_element_type=F32) + b_ref[...])

            a = jnp.minimum(proj(wa_bf, ba_ref), SWIGLU_LIMIT)
            b = jnp.clip(proj(wb_bf, bb_ref), -SWIGLU_LIMIT, SWIGLU_LIMIT)
            o_ref[rows, :] = (a * jax.nn.sigmoid(SWIGLU_ALPHA * a) * (b + 1.0)).astype(o_ref.dtype)

        _on_valid_rows(tr_ref[t], o_ref, compute)


def _gmm1(xg, w1, b1, layer, tile_e, n_used, tile_rows, *, tm, tf):
    p, half = xg.shape
    d = 2 * half
    n_exp, ff = w1.shape[1], w1.shape[3] // 2
    nj = ff // tf
    assert ff % tf == 0 and p % tm == 0 and tm % MOE_SUB_ROWS == 0

    def row(t, nu):
        return jnp.minimum(t, nu[0] - 1)

    return pl.pallas_call(
        _gmm1_kernel,
        out_shape=jax.ShapeDtypeStruct((p, ff), BF16),
        grid_spec=pltpu.PrefetchScalarGridSpec(
            num_scalar_prefetch=3,
            grid=(nj, p // tm),
            in_specs=[
                pl.BlockSpec((tm, half), lambda j, t, te, nu, tr: (row(t, nu), 0)),
                pl.BlockSpec((None, None, d, tf), lambda j, t, te, nu, tr: (layer, te[t], 0, j)),
                pl.BlockSpec((None, None, d, tf),
                             lambda j, t, te, nu, tr: (layer, te[t], 0, nj + j)),
                pl.BlockSpec((None, None, 1, tf), lambda j, t, te, nu, tr: (layer, te[t], 0, j)),
                pl.BlockSpec((None, None, 1, tf),
                             lambda j, t, te, nu, tr: (layer, te[t], 0, nj + j)),
            ],
            out_specs=pl.BlockSpec((tm, tf), lambda j, t, te, nu, tr: (row(t, nu), j)),
            scratch_shapes=[pltpu.VMEM((d, tf), BF16), pltpu.VMEM((d, tf), BF16)],
        ),
        compiler_params=_params("arbitrary", "arbitrary"),
        name="moe_gmm1",
    )(tile_e, n_used, tile_rows, xg, w1, w1, b1.reshape(b1.shape[0], n_exp, 1, 2 * ff),
      b1.reshape(b1.shape[0], n_exp, 1, 2 * ff))


def _gmm2_kernel(te_ref, nu_ref, tr_ref, x_ref, w_ref, b_ref, o_ref, w_bf):
    t = pl.program_id(1)

    @pl.when(t < nu_ref[0])
    def _():
        @pl.when(_is_first_tile_of_expert(te_ref, t))
        def _():
            w_bf[...] = w_ref[...].astype(BF16)

        def compute(rows):
            o_ref[rows, :] = (jnp.dot(x_ref[rows, :], w_bf[...], preferred_element_type=F32)
                              + b_ref[...])

        _on_valid_rows(tr_ref[t], o_ref, compute)


def _gmm2(act, w2, b2, layer, tile_e, n_used, tile_rows, *, tm, tn):
    p, ff = act.shape
    n_exp, d = w2.shape[1], w2.shape[3]
    assert d % tn == 0 and tm % MOE_SUB_ROWS == 0

    def row(t, nu):
        return jnp.minimum(t, nu[0] - 1)

    return pl.pallas_call(
        _gmm2_kernel,
        out_shape=jax.ShapeDtypeStruct((p, d), F32),
        grid_spec=pltpu.PrefetchScalarGridSpec(
            num_scalar_prefetch=3,
            grid=(d // tn, p // tm),
            in_specs=[
                pl.BlockSpec((tm, ff), lambda j, t, te, nu, tr: (row(t, nu), 0)),
                pl.BlockSpec((None, None, ff, tn), lambda j, t, te, nu, tr: (layer, te[t], 0, j)),
                pl.BlockSpec((None, None, 1, tn), lambda j, t, te, nu, tr: (layer, te[t], 0, j)),
            ],
            out_specs=pl.BlockSpec((tm, tn), lambda j, t, te, nu, tr: (row(t, nu), j)),
            scratch_shapes=[pltpu.VMEM((ff, tn), BF16)],
        ),
        compiler_params=_params("arbitrary", "arbitrary"),
        name="moe_gmm2",
    )(tile_e, n_used, tile_rows, act, w2, b2.reshape(b2.shape[0], n_exp, 1, d))


def _combine_kernel(dcur_ref, dnext_ref, x_ref, gate_ref, g_ref, y_ref, o_ref, buf, sem, *,
                    tt, final_norm):
    t = pl.program_id(0)
    slot = lax.rem(t, 2)

    def row_copy(d_ref, r, kk, s):
        return pltpu.make_async_copy(y_ref.at[pl.ds(d_ref[0, 0, r * TOP_K + kk], 1)],
                                     buf.at[s, kk, pl.ds(r, 1)], sem.at[s])

    def issue(d_ref, s):
        for r in range(tt):
            for kk in range(TOP_K):
                row_copy(d_ref, r, kk, s).start(priority=kk % 2)

    @pl.when(t == 0)
    def _():
        issue(dcur_ref, 0)

    @pl.when(t + 1 < pl.num_programs(0))
    def _():
        issue(dnext_ref, 1 - slot)

    for kk in range(TOP_K):
        pltpu.make_async_copy(y_ref.at[pl.ds(0, tt)], buf.at[slot, kk], sem.at[slot]).wait()

    gate = gate_ref[...]
    acc = x_ref[...]
    for kk in range(TOP_K):
        acc = acc + gate[:, kk:kk + 1] * buf[slot, kk]
    if final_norm:
        acc = acc * lax.rsqrt(jnp.mean(acc * acc, axis=-1, keepdims=True) + EPS) * g_ref[...]
    o_ref[...] = acc


def _combine(x, y, dest, gate, g_final, *, tt, final_norm):
    n, d = x.shape
    nt = n // tt
    dest3 = dest.reshape(nt, 1, tt * TOP_K)
    return pl.pallas_call(
        functools.partial(_combine_kernel, tt=tt, final_norm=final_norm),
        out_shape=jax.ShapeDtypeStruct((n, d), F32),
        grid=(nt,),
        in_specs=[
            pl.BlockSpec((1, 1, tt * TOP_K), lambda t: (t, 0, 0), memory_space=pltpu.SMEM),
            pl.BlockSpec((1, 1, tt * TOP_K), lambda t: (jnp.minimum(t + 1, nt - 1), 0, 0),
                         memory_space=pltpu.SMEM),
            pl.BlockSpec((tt, d), lambda t: (t, 0)),
            pl.BlockSpec((tt, TOP_K), lambda t: (t, 0)),
            pl.BlockSpec((1, d), lambda t: (0, 0)),
            pl.BlockSpec(memory_space=pl.ANY),
        ],
        out_specs=pl.BlockSpec((tt, d), lambda t: (t, 0)),
        scratch_shapes=[pltpu.VMEM((2, TOP_K, tt, d), F32), pltpu.SemaphoreType.DMA((2,))],
        compiler_params=_params("arbitrary"),
        name="moe_combine",
    )(dest3, dest3, x, gate, g_final.reshape(1, d), y)


def _routing_tables(top_idx, rank, counts, tm):
    n = top_idx.shape[0]
    n_exp = counts.shape[0]
    n_tiles = (n * TOP_K) // tm + n_exp
    padded = (counts + tm - 1) // tm * tm
    pad_end = jnp.cumsum(padded)
    pad_start = pad_end - padded
    experts = jnp.arange(n_exp, dtype=jnp.int32)
    dest = rank + jnp.sum(jnp.where(top_idx[..., None] == experts, pad_start, 0), axis=-1)
    tile_start = jnp.arange(n_tiles, dtype=jnp.int32) * tm
    tile_e = jnp.minimum(jnp.sum(tile_start[:, None] >= pad_end[None, :], axis=-1),
                         n_exp - 1).astype(jnp.int32)
    n_used = pad_end[-1:] // tm
    own = tile_e[:, None] == experts[None, :]
    tile_rows = jnp.clip(jnp.sum(jnp.where(own, counts + pad_start, 0), axis=-1) - tile_start,
                         0, tm)
    ragged = jnp.where(counts % tm != 0, pad_end // tm - 1, -1)
    tail = n_used + experts
    zero_tiles = jnp.concatenate([ragged, jnp.where(tail < n_tiles, tail, -1)])
    return (dest.astype(jnp.int32), tile_e, n_used.astype(jnp.int32),
            tile_rows.astype(jnp.int32), zero_tiles.astype(jnp.int32), n_tiles * tm)


def _pick(n, pref):
    t = min(n, pref)
    while n % t or t % 8:
        t -= 1
    return t


def kernel(x, mem, g_mix, w_in, lambda_q1, lambda_k1, lambda_q2, lambda_k2, g_subln, w_dw, b_dw,
           g_conv_ln, b_conv_ln, w_out, rel_bias, g_cross, g_mem, w_cq, w_ck, w_cv, w_co, g_ffn,
           w_router, b_router, w1, b1, w2, b2, g_final):
    batch, seq, d = x.shape
    n = batch * seq
    n_mem = mem.shape[1]
    depth = g_mix.shape[0]
    n_heads = rel_bias.shape[1]
    attn_w = n_heads * 2 * DIFF_HEAD_DIM
    conv_w = w_dw.shape[2]
    in_w = w_in.shape[2]
    n_exp = w_router.shape[2]
    assert in_w == 3 * attn_w + 2 * conv_w and (3 * attn_w) % conv_w == 0

    tq = _pick(seq, 512)
    xf = x.reshape(n, d)
    memf = mem.reshape(batch * n_mem, d)
    for l in range(depth):
        lambda_init = 0.8 - 0.6 * math.exp(-0.3 * l)
        tbl, lam = _attn_prep(rel_bias, lambda_q1[l], lambda_k1[l], lambda_q2[l], lambda_k2[l],
                              tb=tq, lambda_init=lambda_init)
        h = _rmsnorm(xf, g_mix[l], rows=_pick(n, 256))
        u = _dense([h], w_in, l, tm=_pick(n, 512), tn=_pick(in_w, 1024), out_dtype=BF16,
                   name="in_proj").reshape(batch, seq, in_w)
        attn = _diff_attention(u, tbl, lam, g_subln[l], batch=batch, seq=seq, n_heads=n_heads,
                               tq=tq, tk=_pick(seq, 1024), lambda_init=lambda_init)
        conv = _conv_module(u, w_dw[l], b_dw[l], g_conv_ln[l], b_conv_ln[l], batch=batch, seq=seq,
                            col_block=3 * attn_w // conv_w, ts=_pick(seq, 256))
        xf = _dense([attn.reshape(n, attn_w), conv.reshape(n, conv_w)], w_out, l,
                    tm=_pick(n, 512), tn=_pick(d, 512), out_dtype=F32, res=xf, name="out_proj")
        m = _rmsnorm(memf, g_mem[l], rows=_pick(batch * n_mem, 256))
        cross_w = w_cq.shape[2]
        qc = _dense([xf], w_cq, l, tm=_pick(n, 256), tn=_pick(cross_w, 1024), out_dtype=BF16,
                    norm_g=g_cross[l], name="cross_q")
        kc = _dense([m], w_ck, l, tm=_pick(batch * n_mem, 512), tn=_pick(cross_w, 512),
                    out_dtype=BF16, name="cross_k")
        vc = _dense([m], w_cv, l, tm=_pick(batch * n_mem, 512), tn=_pick(cross_w, 512),
                    out_dtype=BF16, name="cross_v")
        oc = _cross_attention(qc, kc, vc, batch=batch, seq=seq, n_mem=n_mem, tm=_pick(seq, 512))
        xf = _dense([oc], w_co, l, tm=_pick(n, 512), tn=_pick(d, 2048), out_dtype=F32, res=xf,
                    name="cross_o")
        tm = 512
        hp, top_idx, gate, rank, counts = _router(xf, g_ffn[l], w_router, b_router[l], l,
                                                  tm=_pick(n, 256))
        dest, tile_e, n_used, tile_rows, zero_tiles, p_rows = _routing_tables(
            top_idx, rank, counts[0], tm)
        xg = _dispatch(hp, dest, zero_tiles, p=p_rows, tt=_pick(n, 128), tm=tm)
        ff = w2.shape[2]
        act = _gmm1(xg, w1, b1, l, tile_e, n_used, tile_rows, tm=tm, tf=_pick(ff, 512))
        y = _gmm2(act, w2, b2, l, tile_e, n_used, tile_rows, tm=tm, tn=_pick(d, 2048))
        xf = _combine(xf, y, dest, gate, g_final, tt=_pick(n, 64), final_norm=l == depth - 1)
    return xf.reshape(batch, seq, d)
```

```python
import functools
import math

import jax
import jax.numpy as jnp
from jax import lax
from jax.experimental import pallas as pl
from jax.experimental.pallas import tpu as pltpu

F32 = jnp.float32
BF16 = jnp.bfloat16
U32 = jnp.uint32

DIFF_HEAD_DIM = 128
CONV_KERNEL = 31
N_BUCKETS = 32
MAX_DISTANCE = 128
N_CROSS_HEADS = 4
TOP_K = 4
SWIGLU_LIMIT = 7.0
SWIGLU_ALPHA = 1.702
EPS = 1e-6
LOG2_E = math.log2(math.e)
N_BIAS_TILES = 5

V7X_VMEM_BYTES = 64 * 1024 * 1024
VMEM_LIMIT_BYTES = V7X_VMEM_BYTES - 6 * 1024 * 1024
LANES = 128
COMBINE_SLOTS = 3
COMBINE_GROUP = 8
MOE_SUB_ROWS = 128
CONV_RBLOCK = 64
CONV_CBLOCK = 256
HALO_ROWS = 16

_T5_LOG_THRESHOLDS = tuple(
    math.ceil(8 * (MAX_DISTANCE / 8) ** (k / 8) - 1e-9) for k in range(1, 8))


def _params(*semantics):
    return pltpu.CompilerParams(dimension_semantics=semantics,
                                vmem_limit_bytes=VMEM_LIMIT_BYTES)


def _rmsnorm_kernel(x_ref, g_ref, o_ref):
    x = x_ref[...]
    y = x * lax.rsqrt(jnp.mean(x * x, axis=-1, keepdims=True) + EPS)
    o_ref[...] = (y * g_ref[...]).astype(o_ref.dtype)


def _rmsnorm(x, g, *, rows):
    n, d = x.shape
    return pl.pallas_call(
        _rmsnorm_kernel,
        out_shape=jax.ShapeDtypeStruct((n, d), BF16),
        grid=(n // rows,),
        in_specs=[pl.BlockSpec((rows, d), lambda i: (i, 0)),
                  pl.BlockSpec((1, d), lambda i: (0, 0))],
        out_specs=pl.BlockSpec((rows, d), lambda i: (i, 0)),
        compiler_params=_params("arbitrary"),
        name="rmsnorm",
    )(x, g.reshape(1, d))


def _dense_kernel(*refs, k_splits, has_norm, has_res):
    n_lhs = len(k_splits)
    x_refs = refs[:n_lhs]
    rest = list(refs[n_lhs:-2])
    g_ref = rest.pop(0) if has_norm else None
    w_ref = rest.pop(0)
    res_ref = rest.pop(0) if has_res else None
    o_ref, wbf_ref = refs[-2], refs[-1]

    @pl.when(pl.program_id(1) == 0)
    def _():
        wbf_ref[...] = w_ref[...].astype(BF16)

    acc = None
    k0 = 0
    for x_ref, kw in zip(x_refs, k_splits):
        x = x_ref[...]
        if has_norm:
            x = x * lax.rsqrt(jnp.mean(x * x, axis=-1, keepdims=True) + EPS)
            x = (x * g_ref[...]).astype(BF16)
        part = jnp.dot(x, wbf_ref[k0:k0 + kw, :], preferred_element_type=F32)
        acc = part if acc is None else acc + part
        k0 += kw
    if has_res:
        acc = res_ref[...] + acc
    o_ref[...] = acc.astype(o_ref.dtype)


def _dense(xs, w, layer, *, tm, tn, out_dtype, norm_g=None, res=None, name):
    m = xs[0].shape[0]
    k_splits = tuple(x.shape[1] for x in xs)
    k, n = w.shape[1], w.shape[2]
    assert sum(k_splits) == k and m % tm == 0 and n % tn == 0
    assert norm_g is None or len(xs) == 1
    in_specs = [pl.BlockSpec((tm, kw), lambda j, i: (i, 0)) for kw in k_splits]
    args = list(xs)
    if norm_g is not None:
        in_specs.append(pl.BlockSpec((1, k), lambda j, i: (0, 0)))
        args.append(norm_g.reshape(1, k))
    in_specs.append(pl.BlockSpec((None, k, tn), lambda j, i: (layer, 0, j)))
    args.append(w)
    if res is not None:
        in_specs.append(pl.BlockSpec((tm, tn), lambda j, i: (i, j)))
        args.append(res)
    return pl.pallas_call(
        functools.partial(_dense_kernel, k_splits=k_splits, has_norm=norm_g is not None,
                          has_res=res is not None),
        out_shape=jax.ShapeDtypeStruct((m, n), out_dtype),
        grid=(n // tn, m // tm),
        in_specs=in_specs,
        out_specs=pl.BlockSpec((tm, tn), lambda j, i: (i, j)),
        scratch_shapes=[pltpu.VMEM((k, tn), BF16)],
        compiler_params=_params("arbitrary", "arbitrary"),
        name=name,
    )(*args)


def _t5_bucket(rel):
    half = N_BUCKETS // 2
    max_exact = half // 2
    n = jnp.abs(rel)
    large = jnp.full(rel.shape, max_exact, jnp.int32)
    for thr in _T5_LOG_THRESHOLDS:
        large = large + (n >= thr).astype(jnp.int32)
    return jnp.where(rel > 0, half, 0) + jnp.where(n < max_exact, n, large)


def _t5_bucket_static(rel):
    half = N_BUCKETS // 2
    n = abs(rel)
    large = half // 2 + sum(n >= thr for thr in _T5_LOG_THRESHOLDS)
    return (half if rel > 0 else 0) + (n if n < half // 2 else large)


def _attn_prep_kernel(rb_ref, lq1_ref, lk1_ref, lq2_ref, lk2_ref, tbl_ref, lam_ref, *,
                      tb, lambda_init):
    h = pl.program_id(0)
    a = lax.broadcasted_iota(jnp.int32, (tb, tb), 0)
    b = lax.broadcasted_iota(jnp.int32, (tb, tb), 1)
    for u in range(N_BIAS_TILES):
        origin = (u - N_BIAS_TILES // 2) * tb
        reachable = sorted({_t5_bucket_static(rel)
                            for rel in range(origin - tb + 1, origin + tb)})
        bucket = _t5_bucket(origin + b - a)
        val = jnp.full((tb, tb), rb_ref[reachable[0], h], F32)
        for bk in reachable[1:]:
            val = jnp.where(bucket == bk, rb_ref[bk, h], val)
        tbl_ref[0, u] = val * LOG2_E
    s1 = jnp.sum(lq1_ref[...] * lk1_ref[...], axis=-1, keepdims=True)
    s2 = jnp.sum(lq2_ref[...] * lk2_ref[...], axis=-1, keepdims=True)
    lam_ref[...] = jnp.exp(s1) - jnp.exp(s2) + lambda_init


def _attn_prep(rel_bias, lq1, lk1, lq2, lk2, *, tb, lambda_init):
    n_heads = rel_bias.shape[1]
    assert tb > MAX_DISTANCE
    vec = pl.BlockSpec((1, DIFF_HEAD_DIM), lambda h: (0, 0))
    return pl.pallas_call(
        functools.partial(_attn_prep_kernel, tb=tb, lambda_init=lambda_init),
        out_shape=(jax.ShapeDtypeStruct((n_heads, N_BIAS_TILES, tb, tb), F32),
                   jax.ShapeDtypeStruct((1, 1), F32)),
        grid=(n_heads,),
        in_specs=[pl.BlockSpec(memory_space=pltpu.SMEM), vec, vec, vec, vec],
        out_specs=(pl.BlockSpec((1, N_BIAS_TILES, tb, tb), lambda h: (h, 0, 0, 0)),
                   pl.BlockSpec((1, 1), lambda h: (0, 0))),
        compiler_params=_params("arbitrary"),
        name="attn_prep",
    )(rel_bias, lq1.reshape(1, -1), lk1.reshape(1, -1), lq2.reshape(1, -1), lk2.reshape(1, -1))


def _diff_attn_kernel(lam_ref, q_ref, k_ref, v_ref, tbl_ref, g_ref, o_ref, *,
                      tq, tk, seq, lambda_init):
    i = pl.program_id(2)
    dh = DIFF_HEAD_DIM
    scale = dh ** -0.5 * LOG2_E
    q = q_ref[0]
    first_half = lax.broadcasted_iota(jnp.int32, q.shape, 1) < dh
    zero = jnp.zeros(q.shape, q.dtype)
    qq = jnp.concatenate([jnp.where(first_half, q, zero), jnp.where(first_half, zero, q)], axis=0)
    tiles_per_chunk = tk // tq
    parts = []
    for c in range(seq // tk):
        keys = slice(c * tk, (c + 1) * tk)
        s = lax.dot_general(qq, k_ref[0, keys, :], (((1,), (1,)), ((), ())),
                            preferred_element_type=F32)
        bias = jnp.concatenate(
            [tbl_ref[0, jnp.clip(c * tiles_per_chunk + j - i + N_BIAS_TILES // 2,
                                 0, N_BIAS_TILES - 1)] for j in range(tiles_per_chunk)], axis=1)
        t = s * scale + jnp.concatenate([bias, bias], axis=0)
        m = jnp.max(t, axis=-1, keepdims=True)
        e = jnp.exp2(t - m)
        parts.append((m, jnp.sum(e, axis=-1, keepdims=True),
                      jnp.dot(e.astype(BF16), v_ref[0, keys, :], preferred_element_type=F32)))
    m_all = parts[0][0]
    for m, _, _ in parts[1:]:
        m_all = jnp.maximum(m_all, m)
    l_all = jnp.zeros_like(m_all)
    o = jnp.zeros(parts[0][2].shape, F32)
    for m, l, oc in parts:
        w = jnp.exp2(m - m_all)
        l_all = l_all + w * l
        o = o + w * oc
    o = o * (1.0 / l_all)
    o = o[:tq] - lam_ref[0, 0] * o[tq:]
    y = o * lax.rsqrt(jnp.mean(o * o, axis=-1, keepdims=True) + EPS)
    o_ref[0] = ((y * g_ref[...]) * (1.0 - lambda_init)).astype(o_ref.dtype)


def _diff_attention(u, tbl, lam, g_subln, *, batch, seq, n_heads, tq, tk, lambda_init):
    hw = 2 * DIFF_HEAD_DIM
    assert tbl.shape[2] == tq and tk % tq == 0 and seq % tk == 0
    return pl.pallas_call(
        functools.partial(_diff_attn_kernel, tq=tq, tk=tk, seq=seq, lambda_init=lambda_init),
        out_shape=jax.ShapeDtypeStruct((batch, seq, n_heads * hw), BF16),
        grid=(batch, n_heads, seq // tq),
        in_specs=[
            pl.BlockSpec(memory_space=pltpu.SMEM),
            pl.BlockSpec((1, tq, hw), lambda b, h, i: (b, i, h)),
            pl.BlockSpec((1, seq, hw), lambda b, h, i: (b, 0, n_heads + h)),
            pl.BlockSpec((1, seq, hw), lambda b, h, i: (b, 0, 2 * n_heads + h)),
            pl.BlockSpec((1, N_BIAS_TILES, tq, tq), lambda b, h, i: (h, 0, 0, 0)),
            pl.BlockSpec((1, hw), lambda b, h, i: (0, 0)),
        ],
        out_specs=pl.BlockSpec((1, tq, hw), lambda b, h, i: (b, i, h)),
        compiler_params=_params("arbitrary", "arbitrary", "arbitrary"),
        name="diff_attention",
    )(lam, u, u, u, tbl, g_subln.reshape(1, hw))


def _conv_kernel(a_ref, ap_ref, an_ref, g_ref, gp_ref, gn_ref, w_ref, b_ref, lg_ref, lb_ref,
                 o_ref, z_ref, zs_ref, y_ref, *, ts):
    i = pl.program_id(1)
    last = pl.num_programs(1) - 1

    def glu(a, g):
        return a[0].astype(F32) * jax.nn.sigmoid(g[0].astype(F32))

    z_ref[:HALO_ROWS, :] = glu(ap_ref, gp_ref) * (i > 0).astype(F32)
    z_ref[HALO_ROWS:HALO_ROWS + ts, :] = glu(a_ref, g_ref)
    z_ref[HALO_ROWS + ts:, :] = glu(an_ref, gn_ref) * (i < last).astype(F32)
    first = HALO_ROWS - CONV_KERNEL // 2
    span = zs_ref.shape[1]
    for res in range(1, 8):
        zs_ref[res - 1] = z_ref[res:res + span, :]
    n_cblocks = y_ref.shape[1] // CONV_CBLOCK
    for r0 in range(0, ts, CONV_RBLOCK):
        def channel_block(cb, carry, r0=r0):
            cols = pl.ds(pl.multiple_of(cb * CONV_CBLOCK, CONV_CBLOCK), CONV_CBLOCK)
            acc = jnp.zeros((CONV_RBLOCK, CONV_CBLOCK), F32)
            for t in range(CONV_KERNEL):
                res, base = (first + t) % 8, (first + t) // 8 * 8 + r0
                rows = slice(base, base + CONV_RBLOCK)
                shifted = z_ref[rows, cols] if res == 0 else zs_ref[res - 1, rows, cols]
                acc = acc + w_ref[t:t + 1, cols] * shifted
            y_ref[r0:r0 + CONV_RBLOCK, cols] = acc + b_ref[:, cols]
            return carry
        lax.fori_loop(0, n_cblocks, channel_block, 0)
    y = y_ref[...]
    yc = y - jnp.mean(y, axis=-1, keepdims=True)
    yn = yc * lax.rsqrt(jnp.mean(yc * yc, axis=-1, keepdims=True) + EPS)
    yn = yn * lg_ref[...] + lb_ref[...]
    o_ref[0] = (yn * jax.nn.sigmoid(yn)).astype(o_ref.dtype)


def _conv_module(u, w_dw, b_dw, g_ln, b_ln, *, batch, seq, col_block, ts):
    cw = w_dw.shape[1]
    hb = ts // HALO_ROWS
    n_hblocks = seq // HALO_ROWS

    def main(c):
        return pl.BlockSpec((1, ts, cw), lambda b, i: (b, i, c))

    def prev(c):
        return pl.BlockSpec((1, HALO_ROWS, cw), lambda b, i: (b, jnp.maximum(i * hb - 1, 0), c))

    def nxt(c):
        return pl.BlockSpec((1, HALO_ROWS, cw),
                            lambda b, i: (b, jnp.minimum((i + 1) * hb, n_hblocks - 1), c))

    row = pl.BlockSpec((1, cw), lambda b, i: (0, 0))
    return pl.pallas_call(
        functools.partial(_conv_kernel, ts=ts),
        out_shape=jax.ShapeDtypeStruct((batch, seq, cw), BF16),
        grid=(batch, seq // ts),
        in_specs=[main(col_block), prev(col_block), nxt(col_block),
                  main(col_block + 1), prev(col_block + 1), nxt(col_block + 1),
                  pl.BlockSpec((CONV_KERNEL, cw), lambda b, i: (0, 0)), row, row, row],
        out_specs=pl.BlockSpec((1, ts, cw), lambda b, i: (b, i, 0)),
        scratch_shapes=[pltpu.VMEM((ts + 2 * HALO_ROWS, cw), F32),
                        pltpu.VMEM((7, ts + 2 * HALO_ROWS - 8, cw), F32),
                        pltpu.VMEM((ts, cw), F32)],
        compiler_params=_params("arbitrary", "arbitrary"),
        name="conv_module",
    )(u, u, u, u, u, u, w_dw, b_dw.reshape(1, cw), g_ln.reshape(1, cw), b_ln.reshape(1, cw))


def _cross_attn_kernel(q_ref, k_ref, v_ref, o_ref, *, head_dim):
    scale = head_dim ** -0.5
    for h in range(N_CROSS_HEADS):
        cols = slice(h * head_dim, (h + 1) * head_dim)
        s = lax.dot_general(q_ref[:, cols], k_ref[:, cols], (((1,), (1,)), ((), ())),
                            preferred_element_type=F32) * scale
        e = jnp.exp(s - jnp.max(s, axis=-1, keepdims=True))
        p = e * (1.0 / jnp.sum(e, axis=-1, keepdims=True))
        o_ref[:, cols] = jnp.dot(p.astype(BF16), v_ref[:, cols],
                                 preferred_element_type=F32).astype(o_ref.dtype)


def _cross_attention(q, k, v, *, batch, seq, n_mem, tm):
    width = q.shape[1]
    qb = seq // tm
    return pl.pallas_call(
        functools.partial(_cross_attn_kernel, head_dim=width // N_CROSS_HEADS),
        out_shape=jax.ShapeDtypeStruct(q.shape, BF16),
        grid=(batch, qb),
        in_specs=[pl.BlockSpec((tm, width), lambda b, i: (b * qb + i, 0)),
                  pl.BlockSpec((n_mem, width), lambda b, i: (b, 0)),
                  pl.BlockSpec((n_mem, width), lambda b, i: (b, 0))],
        out_specs=pl.BlockSpec((tm, width), lambda b, i: (b * qb + i, 0)),
        compiler_params=_params("arbitrary", "arbitrary"),
        name="cross_attention",
    )(q, k, v)


def _pack_bf16_pairs(lo_half, hi_half):
    a = pltpu.bitcast(lo_half, U32)
    b = pltpu.bitcast(hi_half, U32)
    return (a & jnp.uint32(0xFFFF0000)) | (b >> 16)


def _unpack_bf16_pairs(words):
    a = pltpu.bitcast(words & jnp.uint32(0xFFFF0000), F32)
    b = pltpu.bitcast(words << 16, F32)
    return a.astype(BF16), b.astype(BF16)


def _router_kernel(x_ref, g_ref, w_ref, b_ref, hp_ref, idx_ref, gate_ref, rank_ref, cnt_ref,
                   seen_ref):
    @pl.when(pl.program_id(0) == 0)
    def _():
        seen_ref[...] = jnp.zeros(seen_ref.shape, F32)

    x = x_ref[...]
    tm, d = x.shape
    y = x * lax.rsqrt(jnp.mean(x * x, axis=-1, keepdims=True) + EPS)
    hb = (y * g_ref[...]).astype(BF16)
    hp_ref[...] = _pack_bf16_pairs(hb[:, :d // 2].astype(F32), hb[:, d // 2:].astype(F32))
    logits = jnp.dot(hb, w_ref[...].astype(BF16), preferred_element_type=F32) + b_ref[...]
    n_exp = logits.shape[1]
    lane = lax.broadcasted_iota(jnp.int32, logits.shape, 1)
    kl = lax.broadcasted_iota(jnp.int32, idx_ref.shape, 1)
    vals = logits
    top_v = jnp.zeros(gate_ref.shape, F32)
    top_i = jnp.zeros(idx_ref.shape, jnp.int32)
    picks = []
    for kk in range(TOP_K):
        mx = jnp.max(vals, axis=-1, keepdims=True)
        sel = jnp.min(jnp.where(vals == mx, lane, n_exp), axis=-1, keepdims=True)
        top_v = jnp.where(kl == kk, mx, top_v)
        top_i = jnp.where(kl == kk, sel, top_i)
        picks.append(lane == sel)
        vals = jnp.where(picks[-1], -jnp.inf, vals)
    e = jnp.exp(top_v - jnp.max(top_v, axis=-1, keepdims=True))
    gate_ref[...] = e / jnp.sum(e, axis=-1, keepdims=True)
    idx_ref[...] = top_i
    member = jnp.zeros(logits.shape, F32)
    for pick in picks:
        member = member + pick.astype(F32)
    earlier = (lax.broadcasted_iota(jnp.int32, (tm, tm), 0)
               > lax.broadcasted_iota(jnp.int32, (tm, tm), 1)).astype(BF16)
    before = seen_ref[...] + jnp.dot(earlier, member.astype(BF16), preferred_element_type=F32)
    rank = jnp.zeros(rank_ref.shape, F32)
    for kk, pick in enumerate(picks):
        rank = jnp.where(kl == kk, jnp.sum(jnp.where(pick, before, 0.0), axis=-1, keepdims=True),
                         rank)
    rank_ref[...] = rank.astype(jnp.int32)
    seen_ref[...] = seen_ref[...] + jnp.sum(member, axis=0, keepdims=True)
    cnt_ref[...] = seen_ref[...].astype(jnp.int32)


def _router(x, g, w_router, b_router, layer, *, tm):
    n, d = x.shape
    n_exp = w_router.shape[2]
    per_tok = pl.BlockSpec((tm, TOP_K), lambda i: (i, 0))
    return pl.pallas_call(
        _router_kernel,
        out_shape=(jax.ShapeDtypeStruct((n, d // 2), U32),
                   jax.ShapeDtypeStruct((n, TOP_K), jnp.int32),
                   jax.ShapeDtypeStruct((n, TOP_K), F32),
                   jax.ShapeDtypeStruct((n, TOP_K), jnp.int32),
                   jax.ShapeDtypeStruct((1, n_exp), jnp.int32)),
        grid=(n // tm,),
        in_specs=[pl.BlockSpec((tm, d), lambda i: (i, 0)),
                  pl.BlockSpec((1, d), lambda i: (0, 0)),
                  pl.BlockSpec((None, d, n_exp), lambda i: (layer, 0, 0)),
                  pl.BlockSpec((1, n_exp), lambda i: (0, 0))],
        out_specs=(pl.BlockSpec((tm, d // 2), lambda i: (i, 0)), per_tok, per_tok, per_tok,
                   pl.BlockSpec((1, n_exp), lambda i: (0, 0))),
        scratch_shapes=[pltpu.VMEM((1, n_exp), F32)],
        compiler_params=_params("arbitrary"),
        name="router",
    )(x, g.reshape(1, d), w_router, b_router.reshape(1, n_exp))


def _dispatch_kernel(zt_ref, dest_ref, src_ref, dst_ref, zero_ref, zsem, sem, *, tt, tm):
    @pl.when(pl.program_id(0) == 0)
    def _():
        zero_ref[...] = jnp.zeros(zero_ref.shape, zero_ref.dtype)

        def zero_copy(i):
            return pltpu.make_async_copy(zero_ref, dst_ref.at[pl.ds(zt_ref[i] * tm, tm)], zsem.at[0])

        def start(i, carry):
            @pl.when(zt_ref[i] >= 0)
            def _():
                zero_copy(i).start()
            return carry

        def wait(i, carry):
            @pl.when(zt_ref[i] >= 0)
            def _():
                zero_copy(i).wait()
            return carry

        lax.fori_loop(0, zt_ref.shape[0], start, 0)
        lax.fori_loop(0, zt_ref.shape[0], wait, 0)

    def row_copy(r, kk):
        return pltpu.make_async_copy(src_ref.at[pl.ds(r, 1)],
                                     dst_ref.at[pl.ds(dest_ref[0, 0, r * TOP_K + kk], 1)],
                                     sem.at[0])

    for r in range(tt):
        for kk in range(TOP_K):
            row_copy(r, kk).start(priority=kk % 2)
    for kk in range(TOP_K):
        pltpu.make_async_copy(src_ref, dst_ref.at[pl.ds(0, tt)], sem.at[0]).wait()


def _dispatch(src, dest, zero_tiles, *, p, tt, tm):
    n, half = src.shape
    nt = n // tt
    return pl.pallas_call(
        functools.partial(_dispatch_kernel, tt=tt, tm=tm),
        out_shape=jax.ShapeDtypeStruct((p, half), src.dtype),
        grid_spec=pltpu.PrefetchScalarGridSpec(
            num_scalar_prefetch=1,
            grid=(nt,),
            in_specs=[pl.BlockSpec((1, 1, tt * TOP_K), lambda t, zt: (t, 0, 0),
                                   memory_space=pltpu.SMEM),
                      pl.BlockSpec((tt, half), lambda t, zt: (t, 0))],
            out_specs=pl.BlockSpec(memory_space=pl.ANY),
            scratch_shapes=[pltpu.VMEM((tm, half), src.dtype),
                            pltpu.SemaphoreType.DMA((1,)), pltpu.SemaphoreType.DMA((1,))],
        ),
        compiler_params=_params("arbitrary"),
        name="moe_dispatch",
    )(zero_tiles, dest.reshape(nt, 1, tt * TOP_K), src)


def _is_first_tile_of_expert(te_ref, t):
    return jnp.logical_or(t == 0, te_ref[t] != te_ref[jnp.maximum(t - 1, 0)])


def _on_valid_rows(valid, o_ref, compute):
    tm = o_ref.shape[0]

    @pl.when(valid == tm)
    def _():
        compute(slice(0, tm))

    @pl.when(valid < tm)
    def _():
        for r0 in range(0, tm, MOE_SUB_ROWS):
            rows = slice(r0, r0 + MOE_SUB_ROWS)

            @pl.when(r0 < valid)
            def _(rows=rows):
                compute(rows)

            @pl.when(r0 >= valid)
            def _(rows=rows):
                o_ref[rows, :] = jnp.zeros((MOE_SUB_ROWS, o_ref.shape[1]), o_ref.dtype)


def _gmm1_kernel(te_ref, nu_ref, tr_ref, x_ref, wa_ref, wb_ref, ba_ref, bb_ref, o_ref,
                 wa_bf, wb_bf):
    t = pl.program_id(1)

    @pl.when(t < nu_ref[0])
    def _():
        @pl.when(_is_first_tile_of_expert(te_ref, t))
        def _():
            wa_bf[...] = wa_ref[...].astype(BF16)
            wb_bf[...] = wb_ref[...].astype(BF16)

        def compute(rows):
            x_lo, x_hi = _unpack_bf16_pairs(x_ref[rows, :])
            half = x_lo.shape[1]

            def proj(w_bf, b_ref):
                return (jnp.dot(x_lo, w_bf[:half, :], preferred_element_type=F32)
                        + jnp.dot(x_hi, w_bf[half:, :], preferred_element_type=F32) + b_ref[...])

            a = jnp.minimum(proj(wa_bf, ba_ref), SWIGLU_LIMIT)
            b = jnp.clip(proj(wb_bf, bb_ref), -SWIGLU_LIMIT, SWIGLU_LIMIT)
            o_ref[rows, :] = (a * jax.nn.sigmoid(SWIGLU_ALPHA * a) * (b + 1.0)).astype(o_ref.dtype)

        _on_valid_rows(tr_ref[t], o_ref, compute)


def _gmm1(xg, w1, b1, layer, tile_e, n_used, tile_rows, *, tm, tf):
    p, half = xg.shape
    d = 2 * half
    n_exp, ff = w1.shape[1], w1.shape[3] // 2
    nj = ff // tf
    assert ff % tf == 0 and p % tm == 0 and tm % MOE_SUB_ROWS == 0

    def row(t, nu):
        return jnp.minimum(t, nu[0] - 1)

    return pl.pallas_call(
        _gmm1_kernel,
        out_shape=jax.ShapeDtypeStruct((p, ff), BF16),
        grid_spec=pltpu.PrefetchScalarGridSpec(
            num_scalar_prefetch=3,
            grid=(nj, p // tm),
            in_specs=[
                pl.BlockSpec((tm, half), lambda j, t, te, nu, tr: (row(t, nu), 0)),
                pl.BlockSpec((None, None, d, tf), lambda j, t, te, nu, tr: (layer, te[t], 0, j)),
                pl.BlockSpec((None, None, d, tf),
                             lambda j, t, te, nu, tr: (layer, te[t], 0, nj + j)),
                pl.BlockSpec((None, None, 1, tf), lambda j, t, te, nu, tr: (layer, te[t], 0, j)),
                pl.BlockSpec((None, None, 1, tf),
                             lambda j, t, te, nu, tr: (layer, te[t], 0, nj + j)),
            ],
            out_specs=pl.BlockSpec((tm, tf), lambda j, t, te, nu, tr: (row(t, nu), j)),
            scratch_shapes=[pltpu.VMEM((d, tf), BF16), pltpu.VMEM((d, tf), BF16)],
        ),
        compiler_params=_params("arbitrary", "arbitrary"),
        name="moe_gmm1",
    )(tile_e, n_used, tile_rows, xg, w1, w1, b1.reshape(b1.shape[0], n_exp, 1, 2 * ff),
      b1.reshape(b1.shape[0], n_exp, 1, 2 * ff))


def _gmm2_kernel(te_ref, nu_ref, tr_ref, x_ref, w_ref, b_ref, o_ref, w_bf):
    t = pl.program_id(1)

    @pl.when(t < nu_ref[0])
    def _():
        @pl.when(_is_first_tile_of_expert(te_ref, t))
        def _():
            w_bf[...] = w_ref[...].astype(BF16)

        def compute(rows):
            o_ref[rows, :] = (jnp.dot(x_ref[rows, :], w_bf[...], preferred_element_type=F32)
                              + b_ref[...])

        _on_valid_rows(tr_ref[t], o_ref, compute)


def _gmm2(act, w2, b2, layer, tile_e, n_used, tile_rows, *, tm, tn):
    p, ff = act.shape
    n_exp, d = w2.shape[1], w2.shape[3]
    assert d % tn == 0 and tm % MOE_SUB_ROWS == 0

    def row(t, nu):
        return jnp.minimum(t, nu[0] - 1)

    return pl.pallas_call(
        _gmm2_kernel,
        out_shape=jax.ShapeDtypeStruct((p, d), F32),
        grid_spec=pltpu.PrefetchScalarGridSpec(
            num_scalar_prefetch=3,
            grid=(d // tn, p // tm),
            in_specs=[
                pl.BlockSpec((tm, ff), lambda j, t, te, nu, tr: (row(t, nu), 0)),
                pl.BlockSpec((None, None, ff, tn), lambda j, t, te, nu, tr: (layer, te[t], 0, j)),
                pl.BlockSpec((None, None, 1, tn), lambda j, t, te, nu, tr: (layer, te[t], 0, j)),
            ],
            out_specs=pl.BlockSpec((tm, tn), lambda j, t, te, nu, tr: (row(t, nu), j)),
            scratch_shapes=[pltpu.VMEM((ff, tn), BF16)],
        ),
        compiler_params=_params("arbitrary", "arbitrary"),
        name="moe_gmm2",
    )(tile_e, n_used, tile_rows, act, w2, b2.reshape(b2.shape[0], n_exp, 1, d))


def _combine_kernel(d0_ref, d1_ref, d2_ref, x_ref, gate_ref, g_ref, y_ref, o_ref, buf, sem, *,
                    tt, final_norm):
    t = pl.program_id(0)
    last = pl.num_programs(0) - 1
    slot = lax.rem(t, COMBINE_SLOTS)
    ahead = lax.rem(t + 2, COMBINE_SLOTS)

    def row_copy(d_ref, r, kk, s):
        return pltpu.make_async_copy(y_ref.at[pl.ds(d_ref[0, 0, r * TOP_K + kk], 1)],
                                     buf.at[s, kk, pl.ds(r, 1)], sem.at[s])

    def issue(d_ref, s, rows):
        for r in rows:
            for kk in range(TOP_K):
                row_copy(d_ref, r, kk, s).start(priority=kk % 2)

    def drain(s):
        for kk in range(TOP_K):
            pltpu.make_async_copy(y_ref.at[pl.ds(0, tt)], buf.at[s, kk], sem.at[s]).wait()

    @pl.when(t == 0)
    def _():
        issue(d0_ref, 0, range(tt))
        issue(d1_ref, 1, range(tt))

    drain(slot)
    gate = gate_ref[...]
    for r0 in range(0, tt, COMBINE_GROUP):
        rows = slice(r0, r0 + COMBINE_GROUP)
        acc = x_ref[rows, :]
        parts = [buf[slot, kk, rows, :] for kk in range(TOP_K)]
        issue(d2_ref, ahead, range(r0, r0 + COMBINE_GROUP))
        for kk in range(TOP_K):
            acc = acc + gate[rows, kk:kk + 1] * parts[kk]
        if final_norm:
            acc = acc * lax.rsqrt(jnp.mean(acc * acc, axis=-1, keepdims=True) + EPS) * g_ref[...]
        o_ref[rows, :] = acc

    @pl.when(t == last)
    def _():
        drain(lax.rem(t + 1, COMBINE_SLOTS))
        drain(ahead)


def _combine(x, y, dest, gate, g_final, *, tt, final_norm):
    n, d = x.shape
    nt = n // tt
    assert nt >= COMBINE_SLOTS and tt % COMBINE_GROUP == 0
    dest3 = dest.reshape(nt, 1, tt * TOP_K)

    def tile(k):
        return pl.BlockSpec((1, 1, tt * TOP_K), lambda t: (jnp.minimum(t + k, nt - 1), 0, 0),
                            memory_space=pltpu.SMEM)

    return pl.pallas_call(
        functools.partial(_combine_kernel, tt=tt, final_norm=final_norm),
        out_shape=jax.ShapeDtypeStruct((n, d), F32),
        grid=(nt,),
        in_specs=[
            tile(0), tile(1), tile(2),
            pl.BlockSpec((tt, d), lambda t: (t, 0)),
            pl.BlockSpec((tt, TOP_K), lambda t: (t, 0)),
            pl.BlockSpec((1, d), lambda t: (0, 0)),
            pl.BlockSpec(memory_space=pl.ANY),
        ],
        out_specs=pl.BlockSpec((tt, d), lambda t: (t, 0)),
        scratch_shapes=[pltpu.VMEM((COMBINE_SLOTS, TOP_K, tt, d), F32),
                        pltpu.SemaphoreType.DMA((COMBINE_SLOTS,))],
        compiler_params=_params("arbitrary"),
        name="moe_combine",
    )(dest3, dest3, dest3, x, gate, g_final.reshape(1, d), y)


def _routing_tables(top_idx, rank, counts, tm):
    n = top_idx.shape[0]
    n_exp = counts.shape[0]
    n_tiles = (n * TOP_K) // tm + n_exp
    padded = (counts + tm - 1) // tm * tm
    pad_end = jnp.cumsum(padded)
    pad_start = pad_end - padded
    experts = jnp.arange(n_exp, dtype=jnp.int32)
    dest = rank + jnp.sum(jnp.where(top_idx[..., None] == experts, pad_start, 0), axis=-1)
    tile_start = jnp.arange(n_tiles, dtype=jnp.int32) * tm
    tile_e = jnp.minimum(jnp.sum(tile_start[:, None] >= pad_end[None, :], axis=-1),
                         n_exp - 1).astype(jnp.int32)
    n_used = pad_end[-1:] // tm
    own = tile_e[:, None] == experts[None, :]
    tile_rows = jnp.clip(jnp.sum(jnp.where(own, counts + pad_start, 0), axis=-1) - tile_start,
                         0, tm)
    ragged = jnp.where(counts % tm != 0, pad_end // tm - 1, -1)
    tail = n_used + experts
    zero_tiles = jnp.concatenate([ragged, jnp.where(tail < n_tiles, tail, -1)])
    return (dest.astype(jnp.int32), tile_e, n_used.astype(jnp.int32),
            tile_rows.astype(jnp.int32), zero_tiles.astype(jnp.int32), n_tiles * tm)


def _pick(n, pref):
    t = min(n, pref)
    while n % t or t % 8:
        t -= 1
    return t


def kernel(x, mem, g_mix, w_in, lambda_q1, lambda_k1, lambda_q2, lambda_k2, g_subln, w_dw, b_dw,
           g_conv_ln, b_conv_ln, w_out, rel_bias, g_cross, g_mem, w_cq, w_ck, w_cv, w_co, g_ffn,
           w_router, b_router, w1, b1, w2, b2, g_final):
    batch, seq, d = x.shape
    n = batch * seq
    n_mem = mem.shape[1]
    depth = g_mix.shape[0]
    n_heads = rel_bias.shape[1]
    attn_w = n_heads * 2 * DIFF_HEAD_DIM
    conv_w = w_dw.shape[2]
    in_w = w_in.shape[2]
    n_exp = w_router.shape[2]
    assert in_w == 3 * attn_w + 2 * conv_w and (3 * attn_w) % conv_w == 0

    tq = _pick(seq, 512)
    xf = x.reshape(n, d)
    memf = mem.reshape(batch * n_mem, d)
    for l in range(depth):
        lambda_init = 0.8 - 0.6 * math.exp(-0.3 * l)
        tbl, lam = _attn_prep(rel_bias, lambda_q1[l], lambda_k1[l], lambda_q2[l], lambda_k2[l],
                              tb=tq, lambda_init=lambda_init)
        h = _rmsnorm(xf, g_mix[l], rows=_pick(n, 256))
        u = _dense([h], w_in, l, tm=_pick(n, 512), tn=_pick(in_w, 1024), out_dtype=BF16,
                   name="in_proj").reshape(batch, seq, in_w)
        attn = _diff_attention(u, tbl, lam, g_subln[l], batch=batch, seq=seq, n_heads=n_heads,
                               tq=tq, tk=_pick(seq, 512), lambda_init=lambda_init)
        conv = _conv_module(u, w_dw[l], b_dw[l], g_conv_ln[l], b_conv_ln[l], batch=batch, seq=seq,
                            col_block=3 * attn_w // conv_w, ts=_pick(seq, 256))
        xf = _dense([attn.reshape(n, attn_w), conv.reshape(n, conv_w)], w_out, l,
                    tm=_pick(n, 1024), tn=_pick(d, 512), out_dtype=F32, res=xf, name="out_proj")
        m = _rmsnorm(memf, g_mem[l], rows=_pick(batch * n_mem, 256))
        cross_w = w_cq.shape[2]
        qc = _dense([xf], w_cq, l, tm=_pick(n, 256), tn=_pick(cross_w, 1024), out_dtype=BF16,
                    norm_g=g_cross[l], name="cross_q")
        kc = _dense([m], w_ck, l, tm=_pick(batch * n_mem, 512), tn=_pick(cross_w, 512),
                    out_dtype=BF16, name="cross_k")
        vc = _dense([m], w_cv, l, tm=_pick(batch * n_mem, 512), tn=_pick(cross_w, 512),
                    out_dtype=BF16, name="cross_v")
        oc = _cross_attention(qc, kc, vc, batch=batch, seq=seq, n_mem=n_mem, tm=_pick(seq, 512))
        xf = _dense([oc], w_co, l, tm=_pick(n, 512), tn=_pick(d, 2048), out_dtype=F32, res=xf,
                    name="cross_o")
        tm = 512
        hp, top_idx, gate, rank, counts = _router(xf, g_ffn[l], w_router, b_router[l], l,
                                                  tm=_pick(n, 256))
        dest, tile_e, n_used, tile_rows, zero_tiles, p_rows = _routing_tables(
            top_idx, rank, counts[0], tm)
        xg = _dispatch(hp, dest, zero_tiles, p=p_rows, tt=_pick(n, 128), tm=tm)
        ff = w2.shape[2]
        act = _gmm1(xg, w1, b1, l, tile_e, n_used, tile_rows, tm=tm, tf=_pick(ff, 512))
        y = _gmm2(act, w2, b2, l, tile_e, n_used, tile_rows, tm=tm, tn=_pick(d, 2048))
        xf = _combine(xf, y, dest, gate, g_final, tt=_pick(n, 64), final_norm=l == depth - 1)
    return xf.reshape(batch, seq, d)
```

```python
import functools
import math

import jax
import jax.numpy as jnp
from jax import lax
from jax.experimental import pallas as pl
from jax.experimental.pallas import tpu as pltpu

F32 = jnp.float32
BF16 = jnp.bfloat16
U32 = jnp.uint32

DIFF_HEAD_DIM = 128
CONV_KERNEL = 31
N_BUCKETS = 32
MAX_DISTANCE = 128
N_CROSS_HEADS = 4
TOP_K = 4
SWIGLU_LIMIT = 7.0
SWIGLU_ALPHA = 1.702
EPS = 1e-6
LOG2_E = math.log2(math.e)
N_BIAS_TILES = 5

V7X_VMEM_BYTES = 64 * 1024 * 1024
VMEM_LIMIT_BYTES = V7X_VMEM_BYTES - 6 * 1024 * 1024
LANES = 128
COMBINE_SLOTS = 3
COMBINE_GROUP = 8
MOE_SUB_ROWS = 128
CONV_RBLOCK = 64
CONV_CBLOCK = 256
HALO_ROWS = 16

_T5_LOG_THRESHOLDS = tuple(
    math.ceil(8 * (MAX_DISTANCE / 8) ** (k / 8) - 1e-9) for k in range(1, 8))


def _params(*semantics):
    return pltpu.CompilerParams(dimension_semantics=semantics,
                                vmem_limit_bytes=VMEM_LIMIT_BYTES)


def _rmsnorm_kernel(x_ref, g_ref, o_ref):
    x = x_ref[...]
    y = x * lax.rsqrt(jnp.mean(x * x, axis=-1, keepdims=True) + EPS)
    o_ref[...] = (y * g_ref[...]).astype(o_ref.dtype)


def _rmsnorm(x, g, *, rows):
    n, d = x.shape
    return pl.pallas_call(
        _rmsnorm_kernel,
        out_shape=jax.ShapeDtypeStruct((n, d), BF16),
        grid=(n // rows,),
        in_specs=[pl.BlockSpec((rows, d), lambda i: (i, 0)),
                  pl.BlockSpec((1, d), lambda i: (0, 0))],
        out_specs=pl.BlockSpec((rows, d), lambda i: (i, 0)),
        compiler_params=_params("arbitrary"),
        name="rmsnorm",
    )(x, g.reshape(1, d))


def _dense_kernel(*refs, k_splits, has_norm, has_res):
    n_lhs = len(k_splits)
    x_refs = refs[:n_lhs]
    rest = list(refs[n_lhs:-2])
    g_ref = rest.pop(0) if has_norm else None
    w_ref = rest.pop(0)
    res_ref = rest.pop(0) if has_res else None
    o_ref, wbf_ref = refs[-2], refs[-1]

    @pl.when(pl.program_id(1) == 0)
    def _():
        wbf_ref[...] = w_ref[...].astype(BF16)

    acc = None
    k0 = 0
    for x_ref, kw in zip(x_refs, k_splits):
        x = x_ref[...]
        if has_norm:
            x = x * lax.rsqrt(jnp.mean(x * x, axis=-1, keepdims=True) + EPS)
            x = (x * g_ref[...]).astype(BF16)
        part = jnp.dot(x, wbf_ref[k0:k0 + kw, :], preferred_element_type=F32)
        acc = part if acc is None else acc + part
        k0 += kw
    if has_res:
        acc = res_ref[...] + acc
    o_ref[...] = acc.astype(o_ref.dtype)


def _dense(xs, w, layer, *, tm, tn, out_dtype, norm_g=None, res=None, name):
    m = xs[0].shape[0]
    k_splits = tuple(x.shape[1] for x in xs)
    k, n = w.shape[1], w.shape[2]
    assert sum(k_splits) == k and m % tm == 0 and n % tn == 0
    assert norm_g is None or len(xs) == 1
    in_specs = [pl.BlockSpec((tm, kw), lambda j, i: (i, 0)) for kw in k_splits]
    args = list(xs)
    if norm_g is not None:
        in_specs.append(pl.BlockSpec((1, k), lambda j, i: (0, 0)))
        args.append(norm_g.reshape(1, k))
    in_specs.append(pl.BlockSpec((None, k, tn), lambda j, i: (layer, 0, j)))
    args.append(w)
    if res is not None:
        in_specs.append(pl.BlockSpec((tm, tn), lambda j, i: (i, j)))
        args.append(res)
    return pl.pallas_call(
        functools.partial(_dense_kernel, k_splits=k_splits, has_norm=norm_g is not None,
                          has_res=res is not None),
        out_shape=jax.ShapeDtypeStruct((m, n), out_dtype),
        grid=(n // tn, m // tm),
        in_specs=in_specs,
        out_specs=pl.BlockSpec((tm, tn), lambda j, i: (i, j)),
        scratch_shapes=[pltpu.VMEM((k, tn), BF16)],
        compiler_params=_params("arbitrary", "arbitrary"),
        name=name,
    )(*args)


def _t5_bucket(rel):
    half = N_BUCKETS // 2
    max_exact = half // 2
    n = jnp.abs(rel)
    large = jnp.full(rel.shape, max_exact, jnp.int32)
    for thr in _T5_LOG_THRESHOLDS:
        large = large + (n >= thr).astype(jnp.int32)
    return jnp.where(rel > 0, half, 0) + jnp.where(n < max_exact, n, large)


def _t5_bucket_static(rel):
    half = N_BUCKETS // 2
    n = abs(rel)
    large = half // 2 + sum(n >= thr for thr in _T5_LOG_THRESHOLDS)
    return (half if rel > 0 else 0) + (n if n < half // 2 else large)


def _attn_prep_kernel(rb_ref, lq1_ref, lk1_ref, lq2_ref, lk2_ref, tbl_ref, lam_ref, *,
                      tb, lambda_init):
    h = pl.program_id(0)
    a = lax.broadcasted_iota(jnp.int32, (tb, tb), 0)
    b = lax.broadcasted_iota(jnp.int32, (tb, tb), 1)
    for u in range(N_BIAS_TILES):
        origin = (u - N_BIAS_TILES // 2) * tb
        reachable = sorted({_t5_bucket_static(rel)
                            for rel in range(origin - tb + 1, origin + tb)})
        bucket = _t5_bucket(origin + b - a)
        val = jnp.full((tb, tb), rb_ref[reachable[0], h], F32)
        for bk in reachable[1:]:
            val = jnp.where(bucket == bk, rb_ref[bk, h], val)
        tbl_ref[0, u] = val * LOG2_E
    s1 = jnp.sum(lq1_ref[...] * lk1_ref[...], axis=-1, keepdims=True)
    s2 = jnp.sum(lq2_ref[...] * lk2_ref[...], axis=-1, keepdims=True)
    lam_ref[...] = jnp.exp(s1) - jnp.exp(s2) + lambda_init


def _attn_prep(rel_bias, lq1, lk1, lq2, lk2, *, tb, lambda_init):
    n_heads = rel_bias.shape[1]
    assert tb > MAX_DISTANCE
    vec = pl.BlockSpec((1, DIFF_HEAD_DIM), lambda h: (0, 0))
    return pl.pallas_call(
        functools.partial(_attn_prep_kernel, tb=tb, lambda_init=lambda_init),
        out_shape=(jax.ShapeDtypeStruct((n_heads, N_BIAS_TILES, tb, tb), F32),
                   jax.ShapeDtypeStruct((1, 1), F32)),
        grid=(n_heads,),
        in_specs=[pl.BlockSpec(memory_space=pltpu.SMEM), vec, vec, vec, vec],
        out_specs=(pl.BlockSpec((1, N_BIAS_TILES, tb, tb), lambda h: (h, 0, 0, 0)),
                   pl.BlockSpec((1, 1), lambda h: (0, 0))),
        compiler_params=_params("arbitrary"),
        name="attn_prep",
    )(rel_bias, lq1.reshape(1, -1), lk1.reshape(1, -1), lq2.reshape(1, -1), lk2.reshape(1, -1))


def _diff_attn_kernel(lam_ref, q_ref, k_ref, v_ref, tbl_ref, g_ref, o_ref, *,
                      tq, tk, seq, lambda_init):
    i = pl.program_id(2)
    dh = DIFF_HEAD_DIM
    scale = dh ** -0.5 * LOG2_E
    q = q_ref[0]
    first_half = lax.broadcasted_iota(jnp.int32, q.shape, 1) < dh
    zero = jnp.zeros(q.shape, q.dtype)
    qq = jnp.concatenate([jnp.where(first_half, q, zero), jnp.where(first_half, zero, q)], axis=0)
    tiles_per_chunk = tk // tq
    parts = []
    for c in range(seq // tk):
        keys = slice(c * tk, (c + 1) * tk)
        s = lax.dot_general(qq, k_ref[0, keys, :], (((1,), (1,)), ((), ())),
                            preferred_element_type=F32)
        bias = jnp.concatenate(
            [tbl_ref[0, jnp.clip(c * tiles_per_chunk + j - i + N_BIAS_TILES // 2,
                                 0, N_BIAS_TILES - 1)] for j in range(tiles_per_chunk)], axis=1)
        t = s * scale + jnp.concatenate([bias, bias], axis=0)
        m = jnp.max(t, axis=-1, keepdims=True)
        e = jnp.exp2(t - m)
        parts.append((m, jnp.sum(e, axis=-1, keepdims=True),
                      jnp.dot(e.astype(BF16), v_ref[0, keys, :], preferred_element_type=F32)))
    m_all = parts[0][0]
    for m, _, _ in parts[1:]:
        m_all = jnp.maximum(m_all, m)
    l_all = jnp.zeros_like(m_all)
    o = jnp.zeros(parts[0][2].shape, F32)
    for m, l, oc in parts:
        w = jnp.exp2(m - m_all)
        l_all = l_all + w * l
        o = o + w * oc
    o = o * (1.0 / l_all)
    o = o[:tq] - lam_ref[0, 0] * o[tq:]
    y = o * lax.rsqrt(jnp.mean(o * o, axis=-1, keepdims=True) + EPS)
    o_ref[0] = ((y * g_ref[...]) * (1.0 - lambda_init)).astype(o_ref.dtype)


def _diff_attention(u, tbl, lam, g_subln, *, batch, seq, n_heads, tq, tk, lambda_init):
    hw = 2 * DIFF_HEAD_DIM
    assert tbl.shape[2] == tq and tk % tq == 0 and seq % tk == 0
    return pl.pallas_call(
        functools.partial(_diff_attn_kernel, tq=tq, tk=tk, seq=seq, lambda_init=lambda_init),
        out_shape=jax.ShapeDtypeStruct((batch, seq, n_heads * hw), BF16),
        grid=(batch, n_heads, seq // tq),
        in_specs=[
            pl.BlockSpec(memory_space=pltpu.SMEM),
            pl.BlockSpec((1, tq, hw), lambda b, h, i: (b, i, h)),
            pl.BlockSpec((1, seq, hw), lambda b, h, i: (b, 0, n_heads + h)),
            pl.BlockSpec((1, seq, hw), lambda b, h, i: (b, 0, 2 * n_heads + h)),
            pl.BlockSpec((1, N_BIAS_TILES, tq, tq), lambda b, h, i: (h, 0, 0, 0)),
            pl.BlockSpec((1, hw), lambda b, h, i: (0, 0)),
        ],
        out_specs=pl.BlockSpec((1, tq, hw), lambda b, h, i: (b, i, h)),
        compiler_params=_params("arbitrary", "arbitrary", "arbitrary"),
        name="diff_attention",
    )(lam, u, u, u, tbl, g_subln.reshape(1, hw))


def _conv_kernel(a_ref, ap_ref, an_ref, g_ref, gp_ref, gn_ref, w_ref, b_ref, lg_ref, lb_ref,
                 o_ref, z_ref, zs_ref, y_ref, *, ts):
    i = pl.program_id(1)
    last = pl.num_programs(1) - 1

    def glu(a, g):
        return a[0].astype(F32) * jax.nn.sigmoid(g[0].astype(F32))

    z_ref[:HALO_ROWS, :] = glu(ap_ref, gp_ref) * (i > 0).astype(F32)
    z_ref[HALO_ROWS:HALO_ROWS + ts, :] = glu(a_ref, g_ref)
    z_ref[HALO_ROWS + ts:, :] = glu(an_ref, gn_ref) * (i < last).astype(F32)
    first = HALO_ROWS - CONV_KERNEL // 2
    span = zs_ref.shape[1]
    for res in range(1, 8):
        zs_ref[res - 1] = z_ref[res:res + span, :]
    n_cblocks = y_ref.shape[1] // CONV_CBLOCK
    for r0 in range(0, ts, CONV_RBLOCK):
        def channel_block(cb, carry, r0=r0):
            cols = pl.ds(pl.multiple_of(cb * CONV_CBLOCK, CONV_CBLOCK), CONV_CBLOCK)
            acc = jnp.zeros((CONV_RBLOCK, CONV_CBLOCK), F32)
            for t in range(CONV_KERNEL):
                res, base = (first + t) % 8, (first + t) // 8 * 8 + r0
                rows = slice(base, base + CONV_RBLOCK)
                shifted = z_ref[rows, cols] if res == 0 else zs_ref[res - 1, rows, cols]
                acc = acc + w_ref[t:t + 1, cols] * shifted
            y_ref[r0:r0 + CONV_RBLOCK, cols] = acc + b_ref[:, cols]
            return carry
        lax.fori_loop(0, n_cblocks, channel_block, 0)
    y = y_ref[...]
    yc = y - jnp.mean(y, axis=-1, keepdims=True)
    yn = yc * lax.rsqrt(jnp.mean(yc * yc, axis=-1, keepdims=True) + EPS)
    yn = yn * lg_ref[...] + lb_ref[...]
    o_ref[0] = (yn * jax.nn.sigmoid(yn)).astype(o_ref.dtype)


def _conv_module(u, w_dw, b_dw, g_ln, b_ln, *, batch, seq, col_block, ts):
    cw = w_dw.shape[1]
    hb = ts // HALO_ROWS
    n_hblocks = seq // HALO_ROWS

    def main(c):
        return pl.BlockSpec((1, ts, cw), lambda b, i: (b, i, c))

    def prev(c):
        return pl.BlockSpec((1, HALO_ROWS, cw), lambda b, i: (b, jnp.maximum(i * hb - 1, 0), c))

    def nxt(c):
        return pl.BlockSpec((1, HALO_ROWS, cw),
                            lambda b, i: (b, jnp.minimum((i + 1) * hb, n_hblocks - 1), c))

    row = pl.BlockSpec((1, cw), lambda b, i: (0, 0))
    return pl.pallas_call(
        functools.partial(_conv_kernel, ts=ts),
        out_shape=jax.ShapeDtypeStruct((batch, seq, cw), BF16),
        grid=(batch, seq // ts),
        in_specs=[main(col_block), prev(col_block), nxt(col_block),
                  main(col_block + 1), prev(col_block + 1), nxt(col_block + 1),
                  pl.BlockSpec((CONV_KERNEL, cw), lambda b, i: (0, 0)), row, row, row],
        out_specs=pl.BlockSpec((1, ts, cw), lambda b, i: (b, i, 0)),
        scratch_shapes=[pltpu.VMEM((ts + 2 * HALO_ROWS, cw), F32),
                        pltpu.VMEM((7, ts + 2 * HALO_ROWS - 8, cw), F32),
                        pltpu.VMEM((ts, cw), F32)],
        compiler_params=_params("arbitrary", "arbitrary"),
        name="conv_module",
    )(u, u, u, u, u, u, w_dw, b_dw.reshape(1, cw), g_ln.reshape(1, cw), b_ln.reshape(1, cw))


def _cross_attn_kernel(q_ref, k_ref, v_ref, o_ref, *, head_dim):
    scale = head_dim ** -0.5
    for h in range(N_CROSS_HEADS):
        cols = slice(h * head_dim, (h + 1) * head_dim)
        s = lax.dot_general(q_ref[:, cols], k_ref[:, cols], (((1,), (1,)), ((), ())),
                            preferred_element_type=F32) * scale
        e = jnp.exp(s - jnp.max(s, axis=-1, keepdims=True))
        p = e * (1.0 / jnp.sum(e, axis=-1, keepdims=True))
        o_ref[:, cols] = jnp.dot(p.astype(BF16), v_ref[:, cols],
                                 preferred_element_type=F32).astype(o_ref.dtype)


def _cross_attention(q, k, v, *, batch, seq, n_mem, tm):
    width = q.shape[1]
    qb = seq // tm
    return pl.pallas_call(
        functools.partial(_cross_attn_kernel, head_dim=width // N_CROSS_HEADS),
        out_shape=jax.ShapeDtypeStruct(q.shape, BF16),
        grid=(batch, qb),
        in_specs=[pl.BlockSpec((tm, width), lambda b, i: (b * qb + i, 0)),
                  pl.BlockSpec((n_mem, width), lambda b, i: (b, 0)),
                  pl.BlockSpec((n_mem, width), lambda b, i: (b, 0))],
        out_specs=pl.BlockSpec((tm, width), lambda b, i: (b * qb + i, 0)),
        compiler_params=_params("arbitrary", "arbitrary"),
        name="cross_attention",
    )(q, k, v)


def _pack_bf16_pairs(lo_half, hi_half):
    a = pltpu.bitcast(lo_half, U32)
    b = pltpu.bitcast(hi_half, U32)
    return (a & jnp.uint32(0xFFFF0000)) | (b >> 16)


def _unpack_bf16_pairs(words):
    a = pltpu.bitcast(words & jnp.uint32(0xFFFF0000), F32)
    b = pltpu.bitcast(words << 16, F32)
    return a.astype(BF16), b.astype(BF16)


def _router_kernel(x_ref, g_ref, w_ref, b_ref, hp_ref, idx_ref, gate_ref, rank_ref, cnt_ref,
                   seen_ref):
    @pl.when(pl.program_id(0) == 0)
    def _():
        seen_ref[...] = jnp.zeros(seen_ref.shape, F32)

    x = x_ref[...]
    tm, d = x.shape
    y = x * lax.rsqrt(jnp.mean(x * x, axis=-1, keepdims=True) + EPS)
    hb = (y * g_ref[...]).astype(BF16)
    hp_ref[...] = _pack_bf16_pairs(hb[:, :d // 2].astype(F32), hb[:, d // 2:].astype(F32))
    logits = jnp.dot(hb, w_ref[...].astype(BF16), preferred_element_type=F32) + b_ref[...]
    n_exp = logits.shape[1]
    lane = lax.broadcasted_iota(jnp.int32, logits.shape, 1)
    kl = lax.broadcasted_iota(jnp.int32, idx_ref.shape, 1)
    vals = logits
    top_v = jnp.zeros(gate_ref.shape, F32)
    top_i = jnp.zeros(idx_ref.shape, jnp.int32)
    picks = []
    for kk in range(TOP_K):
        mx = jnp.max(vals, axis=-1, keepdims=True)
        sel = jnp.min(jnp.where(vals == mx, lane, n_exp), axis=-1, keepdims=True)
        top_v = jnp.where(kl == kk, mx, top_v)
        top_i = jnp.where(kl == kk, sel, top_i)
        picks.append(lane == sel)
        vals = jnp.where(picks[-1], -jnp.inf, vals)
    e = jnp.exp(top_v - jnp.max(top_v, axis=-1, keepdims=True))
    gate_ref[...] = e / jnp.sum(e, axis=-1, keepdims=True)
    idx_ref[...] = top_i
    member = jnp.zeros(logits.shape, F32)
    for pick in picks:
        member = member + pick.astype(F32)
    earlier = (lax.broadcasted_iota(jnp.int32, (tm, tm), 0)
               > lax.broadcasted_iota(jnp.int32, (tm, tm), 1)).astype(BF16)
    before = seen_ref[...] + jnp.dot(earlier, member.astype(BF16), preferred_element_type=F32)
    rank = jnp.zeros(rank_ref.shape, F32)
    for kk, pick in enumerate(picks):
        rank = jnp.where(kl == kk, jnp.sum(jnp.where(pick, before, 0.0), axis=-1, keepdims=True),
                         rank)
    rank_ref[...] = rank.astype(jnp.int32)
    seen_ref[...] = seen_ref[...] + jnp.sum(member, axis=0, keepdims=True)
    cnt_ref[...] = seen_ref[...].astype(jnp.int32)


def _router(x, g, w_router, b_router, layer, *, tm):
    n, d = x.shape
    n_exp = w_router.shape[2]
    per_tok = pl.BlockSpec((tm, TOP_K), lambda i: (i, 0))
    return pl.pallas_call(
        _router_kernel,
        out_shape=(jax.ShapeDtypeStruct((n, d // 2), U32),
                   jax.ShapeDtypeStruct((n, TOP_K), jnp.int32),
                   jax.ShapeDtypeStruct((n, TOP_K), F32),
                   jax.ShapeDtypeStruct((n, TOP_K), jnp.int32),
                   jax.ShapeDtypeStruct((1, n_exp), jnp.int32)),
        grid=(n // tm,),
        in_specs=[pl.BlockSpec((tm, d), lambda i: (i, 0)),
                  pl.BlockSpec((1, d), lambda i: (0, 0)),
                  pl.BlockSpec((None, d, n_exp), lambda i: (layer, 0, 0)),
                  pl.BlockSpec((1, n_exp), lambda i: (0, 0))],
        out_specs=(pl.BlockSpec((tm, d // 2), lambda i: (i, 0)), per_tok, per_tok, per_tok,
                   pl.BlockSpec((1, n_exp), lambda i: (0, 0))),
        scratch_shapes=[pltpu.VMEM((1, n_exp), F32)],
        compiler_params=_params("arbitrary"),
        name="router",
    )(x, g.reshape(1, d), w_router, b_router.reshape(1, n_exp))


def _dispatch_kernel(zt_ref, dest_ref, src_ref, dst_ref, zero_ref, zsem, sem, *, tt, tm):
    @pl.when(pl.program_id(0) == 0)
    def _():
        zero_ref[...] = jnp.zeros(zero_ref.shape, zero_ref.dtype)

        def zero_copy(i):
            return pltpu.make_async_copy(zero_ref, dst_ref.at[pl.ds(zt_ref[i] * tm, tm)], zsem.at[0])

        def start(i, carry):
            @pl.when(zt_ref[i] >= 0)
            def _():
                zero_copy(i).start()
            return carry

        def wait(i, carry):
            @pl.when(zt_ref[i] >= 0)
            def _():
                zero_copy(i).wait()
            return carry

        lax.fori_loop(0, zt_ref.shape[0], start, 0)
        lax.fori_loop(0, zt_ref.shape[0], wait, 0)

    def row_copy(r, kk):
        return pltpu.make_async_copy(src_ref.at[pl.ds(r, 1)],
                                     dst_ref.at[pl.ds(dest_ref[0, 0, r * TOP_K + kk], 1)],
                                     sem.at[0])

    for r in range(tt):
        for kk in range(TOP_K):
            row_copy(r, kk).start(priority=kk % 2)
    for kk in range(TOP_K):
        pltpu.make_async_copy(src_ref, dst_ref.at[pl.ds(0, tt)], sem.at[0]).wait()


def _dispatch(src, dest, zero_tiles, *, p, tt, tm):
    n, half = src.shape
    nt = n // tt
    return pl.pallas_call(
        functools.partial(_dispatch_kernel, tt=tt, tm=tm),
        out_shape=jax.ShapeDtypeStruct((p, half), src.dtype),
        grid_spec=pltpu.PrefetchScalarGridSpec(
            num_scalar_prefetch=1,
            grid=(nt,),
            in_specs=[pl.BlockSpec((1, 1, tt * TOP_K), lambda t, zt: (t, 0, 0),
                                   memory_space=pltpu.SMEM),
                      pl.BlockSpec((tt, half), lambda t, zt: (t, 0))],
            out_specs=pl.BlockSpec(memory_space=pl.ANY),
            scratch_shapes=[pltpu.VMEM((tm, half), src.dtype),
                            pltpu.SemaphoreType.DMA((1,)), pltpu.SemaphoreType.DMA((1,))],
        ),
        compiler_params=_params("arbitrary"),
        name="moe_dispatch",
    )(zero_tiles, dest.reshape(nt, 1, tt * TOP_K), src)


def _is_first_tile_of_expert(te_ref, t):
    return jnp.logical_or(t == 0, te_ref[t] != te_ref[jnp.maximum(t - 1, 0)])


def _on_valid_rows(valid, o_ref, compute):
    tm = o_ref.shape[0]

    @pl.when(valid == tm)
    def _():
        compute(slice(0, tm))

    @pl.when(valid < tm)
    def _():
        for r0 in range(0, tm, MOE_SUB_ROWS):
            rows = slice(r0, r0 + MOE_SUB_ROWS)

            @pl.when(r0 < valid)
            def _(rows=rows):
                compute(rows)

            @pl.when(r0 >= valid)
            def _(rows=rows):
                o_ref[rows, :] = jnp.zeros((MOE_SUB_ROWS, o_ref.shape[1]), o_ref.dtype)


def _gmm1_kernel(te_ref, nu_ref, tr_ref, x_ref, wa_ref, wb_ref, ba_ref, bb_ref, o_ref,
                 wa_bf, wb_bf):
    t = pl.program_id(1)

    @pl.when(t < nu_ref[0])
    def _():
        @pl.when(_is_first_tile_of_expert(te_ref, t))
        def _():
            wa_bf[...] = wa_ref[...].astype(BF16)
            wb_bf[...] = wb_ref[...].astype(BF16)

        def compute(rows):
            x_lo, x_hi = _unpack_bf16_pairs(x_ref[rows, :])
            half = x_lo.shape[1]

            def proj(w_bf, b_ref):
                return (jnp.dot(x_lo, w_bf[:half, :], preferred_element_type=F32)
                        + jnp.dot(x_hi, w_bf[half:, :], preferred_element_type=F32) + b_ref[...])

            a = jnp.minimum(proj(wa_bf, ba_ref), SWIGLU_LIMIT)
            b = jnp.clip(proj(wb_bf, bb_ref), -SWIGLU_LIMIT, SWIGLU_LIMIT)
            o_ref[rows, :] = (a * jax.nn.sigmoid(SWIGLU_ALPHA * a) * (b + 1.0)).astype(o_ref.dtype)

        _on_valid_rows(tr_ref[t], o_ref, compute)


def _gmm1(xg, w1, b1, layer, tile_e, n_used, tile_rows, *, tm, tf):
    p, half = xg.shape
    d = 2 * half
    n_exp, ff = w1.shape[1], w1.shape[3] // 2
    nj = ff // tf
    assert ff % tf == 0 and p % tm == 0 and tm % MOE_SUB_ROWS == 0

    def row(t, nu):
        return jnp.minimum(t, nu[0] - 1)

    return pl.pallas_call(
        _gmm1_kernel,
        out_shape=jax.ShapeDtypeStruct((p, ff), BF16),
        grid_spec=pltpu.PrefetchScalarGridSpec(
            num_scalar_prefetch=3,
            grid=(nj, p // tm),
            in_specs=[
                pl.BlockSpec((tm, half), lambda j, t, te, nu, tr: (row(t, nu), 0)),
                pl.BlockSpec((None, None, d, tf), lambda j, t, te, nu, tr: (layer, te[t], 0, j)),
                pl.BlockSpec((None, None, d, tf),
                             lambda j, t, te, nu, tr: (layer, te[t], 0, nj + j)),
                pl.BlockSpec((None, None, 1, tf), lambda j, t, te, nu, tr: (layer, te[t], 0, j)),
                pl.BlockSpec((None, None, 1, tf),
                             lambda j, t, te, nu, tr: (layer, te[t], 0, nj + j)),
            ],
            out_specs=pl.BlockSpec((tm, tf), lambda j, t, te, nu, tr: (row(t, nu), j)),
            scratch_shapes=[pltpu.VMEM((d, tf), BF16), pltpu.VMEM((d, tf), BF16)],
        ),
        compiler_params=_params("arbitrary", "arbitrary"),
        name="moe_gmm1",
    )(tile_e, n_used, tile_rows, xg, w1, w1, b1.reshape(b1.shape[0], n_exp, 1, 2 * ff),
      b1.reshape(b1.shape[0], n_exp, 1, 2 * ff))


def _gmm2_kernel(te_ref, nu_ref, tr_ref, x_ref, w_ref, b_ref, o_ref, w_bf):
    t = pl.program_id(1)

    @pl.when(t < nu_ref[0])
    def _():
        @pl.when(_is_first_tile_of_expert(te_ref, t))
        def _():
            w_bf[...] = w_ref[...].astype(BF16)

        def compute(rows):
            y = jnp.dot(x_ref[rows, :], w_bf[...], preferred_element_type=F32) + b_ref[...]
            yb = y.astype(BF16).astype(F32)
            half = yb.shape[1] // 2
            o_ref[rows, :] = _pack_bf16_pairs(yb[:, :half], yb[:, half:])

        _on_valid_rows(tr_ref[t], o_ref, compute)


def _gmm2(act, w2, b2, layer, tile_e, n_used, tile_rows, *, tm, tn):
    p, ff = act.shape
    n_exp, d = w2.shape[1], w2.shape[3]
    assert d % tn == 0 and tm % MOE_SUB_ROWS == 0

    def row(t, nu):
        return jnp.minimum(t, nu[0] - 1)

    return pl.pallas_call(
        _gmm2_kernel,
        out_shape=jax.ShapeDtypeStruct((p, d // 2), U32),
        grid_spec=pltpu.PrefetchScalarGridSpec(
            num_scalar_prefetch=3,
            grid=(d // tn, p // tm),
            in_specs=[
                pl.BlockSpec((tm, ff), lambda j, t, te, nu, tr: (row(t, nu), 0)),
                pl.BlockSpec((None, None, ff, tn), lambda j, t, te, nu, tr: (layer, te[t], 0, j)),
                pl.BlockSpec((None, None, 1, tn), lambda j, t, te, nu, tr: (layer, te[t], 0, j)),
            ],
            out_specs=pl.BlockSpec((tm, tn // 2), lambda j, t, te, nu, tr: (row(t, nu), j)),
            scratch_shapes=[pltpu.VMEM((ff, tn), BF16)],
        ),
        compiler_params=_params("arbitrary", "arbitrary"),
        name="moe_gmm2",
    )(tile_e, n_used, tile_rows, act, w2, b2.reshape(b2.shape[0], n_exp, 1, d))


def _unpack_expert_rows(words, pack_cols):
    lo = pltpu.bitcast(words & jnp.uint32(0xFFFF0000), F32)
    hi = pltpu.bitcast(words << 16, F32)
    half = pack_cols // 2
    pieces = []
    for c in range(0, words.shape[1], half):
        pieces += [lo[:, c:c + half], hi[:, c:c + half]]
    return jnp.concatenate(pieces, axis=1)


def _combine_kernel(d0_ref, d1_ref, d2_ref, x_ref, gate_ref, g_ref, y_ref, o_ref, buf, sem, *,
                    tt, pack_cols, final_norm):
    t = pl.program_id(0)
    last = pl.num_programs(0) - 1
    slot = lax.rem(t, COMBINE_SLOTS)
    ahead = lax.rem(t + 2, COMBINE_SLOTS)

    def row_copy(d_ref, r, kk, s):
        return pltpu.make_async_copy(y_ref.at[pl.ds(d_ref[0, 0, r * TOP_K + kk], 1)],
                                     buf.at[s, kk, pl.ds(r, 1)], sem.at[s])

    def issue(d_ref, s, rows):
        for r in rows:
            for kk in range(TOP_K):
                row_copy(d_ref, r, kk, s).start(priority=kk % 2)

    def drain(s):
        for kk in range(TOP_K):
            pltpu.make_async_copy(y_ref.at[pl.ds(0, tt)], buf.at[s, kk], sem.at[s]).wait()

    @pl.when(t == 0)
    def _():
        issue(d0_ref, 0, range(tt))
        issue(d1_ref, 1, range(tt))

    drain(slot)
    gate = gate_ref[...]
    for r0 in range(0, tt, COMBINE_GROUP):
        rows = slice(r0, r0 + COMBINE_GROUP)
        acc = x_ref[rows, :]
        parts = [_unpack_expert_rows(buf[slot, kk, rows, :], pack_cols) for kk in range(TOP_K)]
        issue(d2_ref, ahead, range(r0, r0 + COMBINE_GROUP))
        for kk in range(TOP_K):
            acc = acc + gate[rows, kk:kk + 1] * parts[kk]
        if final_norm:
            acc = acc * lax.rsqrt(jnp.mean(acc * acc, axis=-1, keepdims=True) + EPS) * g_ref[...]
        o_ref[rows, :] = acc

    @pl.when(t == last)
    def _():
        drain(lax.rem(t + 1, COMBINE_SLOTS))
        drain(ahead)


def _combine(x, y, dest, gate, g_final, *, tt, pack_cols, final_norm):
    n, d = x.shape
    nt = n // tt
    assert nt >= COMBINE_SLOTS and tt % COMBINE_GROUP == 0
    dest3 = dest.reshape(nt, 1, tt * TOP_K)

    def tile(k):
        return pl.BlockSpec((1, 1, tt * TOP_K), lambda t: (jnp.minimum(t + k, nt - 1), 0, 0),
                            memory_space=pltpu.SMEM)

    return pl.pallas_call(
        functools.partial(_combine_kernel, tt=tt, pack_cols=pack_cols, final_norm=final_norm),
        out_shape=jax.ShapeDtypeStruct((n, d), F32),
        grid=(nt,),
        in_specs=[
            tile(0), tile(1), tile(2),
            pl.BlockSpec((tt, d), lambda t: (t, 0)),
            pl.BlockSpec((tt, TOP_K), lambda t: (t, 0)),
            pl.BlockSpec((1, d), lambda t: (0, 0)),
            pl.BlockSpec(memory_space=pl.ANY),
        ],
        out_specs=pl.BlockSpec((tt, d), lambda t: (t, 0)),
        scratch_shapes=[pltpu.VMEM((COMBINE_SLOTS, TOP_K, tt, d // 2), U32),
                        pltpu.SemaphoreType.DMA((COMBINE_SLOTS,))],
        compiler_params=_params("arbitrary"),
        name="moe_combine",
    )(dest3, dest3, dest3, x, gate, g_final.reshape(1, d), y)


def _routing_tables(top_idx, rank, counts, tm):
    n = top_idx.shape[0]
    n_exp = counts.shape[0]
    n_tiles = (n * TOP_K) // tm + n_exp
    padded = (counts + tm - 1) // tm * tm
    pad_end = jnp.cumsum(padded)
    pad_start = pad_end - padded
    experts = jnp.arange(n_exp, dtype=jnp.int32)
    dest = rank + jnp.sum(jnp.where(top_idx[..., None] == experts, pad_start, 0), axis=-1)
    tile_start = jnp.arange(n_tiles, dtype=jnp.int32) * tm
    tile_e = jnp.minimum(jnp.sum(tile_start[:, None] >= pad_end[None, :], axis=-1),
                         n_exp - 1).astype(jnp.int32)
    n_used = pad_end[-1:] // tm
    own = tile_e[:, None] == experts[None, :]
    tile_rows = jnp.clip(jnp.sum(jnp.where(own, counts + pad_start, 0), axis=-1) - tile_start,
                         0, tm)
    ragged = jnp.where(counts % tm != 0, pad_end // tm - 1, -1)
    tail = n_used + experts
    zero_tiles = jnp.concatenate([ragged, jnp.where(tail < n_tiles, tail, -1)])
    return (dest.astype(jnp.int32), tile_e, n_used.astype(jnp.int32),
            tile_rows.astype(jnp.int32), zero_tiles.astype(jnp.int32), n_tiles * tm)


def _pick(n, pref):
    t = min(n, pref)
    while n % t or t % 8:
        t -= 1
    return t


def kernel(x, mem, g_mix, w_in, lambda_q1, lambda_k1, lambda_q2, lambda_k2, g_subln, w_dw, b_dw,
           g_conv_ln, b_conv_ln, w_out, rel_bias, g_cross, g_mem, w_cq, w_ck, w_cv, w_co, g_ffn,
           w_router, b_router, w1, b1, w2, b2, g_final):
    batch, seq, d = x.shape
    n = batch * seq
    n_mem = mem.shape[1]
    depth = g_mix.shape[0]
    n_heads = rel_bias.shape[1]
    attn_w = n_heads * 2 * DIFF_HEAD_DIM
    conv_w = w_dw.shape[2]
    in_w = w_in.shape[2]
    n_exp = w_router.shape[2]
    assert in_w == 3 * attn_w + 2 * conv_w and (3 * attn_w) % conv_w == 0

    tq = _pick(seq, 512)
    xf = x.reshape(n, d)
    memf = mem.reshape(batch * n_mem, d)
    for l in range(depth):
        lambda_init = 0.8 - 0.6 * math.exp(-0.3 * l)
        tbl, lam = _attn_prep(rel_bias, lambda_q1[l], lambda_k1[l], lambda_q2[l], lambda_k2[l],
                              tb=tq, lambda_init=lambda_init)
        h = _rmsnorm(xf, g_mix[l], rows=_pick(n, 256))
        u = _dense([h], w_in, l, tm=_pick(n, 512), tn=_pick(in_w, 1024), out_dtype=BF16,
                   name="in_proj").reshape(batch, seq, in_w)
        attn = _diff_attention(u, tbl, lam, g_subln[l], batch=batch, seq=seq, n_heads=n_heads,
                               tq=tq, tk=_pick(seq, 512), lambda_init=lambda_init)
        conv = _conv_module(u, w_dw[l], b_dw[l], g_conv_ln[l], b_conv_ln[l], batch=batch, seq=seq,
                            col_block=3 * attn_w // conv_w, ts=_pick(seq, 256))
        xf = _dense([attn.reshape(n, attn_w), conv.reshape(n, conv_w)], w_out, l,
                    tm=_pick(n, 1024), tn=_pick(d, 512), out_dtype=F32, res=xf, name="out_proj")
        m = _rmsnorm(memf, g_mem[l], rows=_pick(batch * n_mem, 256))
        cross_w = w_cq.shape[2]
        qc = _dense([xf], w_cq, l, tm=_pick(n, 256), tn=_pick(cross_w, 1024), out_dtype=BF16,
                    norm_g=g_cross[l], name="cross_q")
        kc = _dense([m], w_ck, l, tm=_pick(batch * n_mem, 512), tn=_pick(cross_w, 512),
                    out_dtype=BF16, name="cross_k")
        vc = _dense([m], w_cv, l, tm=_pick(batch * n_mem, 512), tn=_pick(cross_w, 512),
                    out_dtype=BF16, name="cross_v")
        oc = _cross_attention(qc, kc, vc, batch=batch, seq=seq, n_mem=n_mem, tm=_pick(seq, 512))
        xf = _dense([oc], w_co, l, tm=_pick(n, 512), tn=_pick(d, 2048), out_dtype=F32, res=xf,
                    name="cross_o")
        tm = 512
        hp, top_idx, gate, rank, counts = _router(xf, g_ffn[l], w_router, b_router[l], l,
                                                  tm=_pick(n, 256))
        dest, tile_e, n_used, tile_rows, zero_tiles, p_rows = _routing_tables(
            top_idx, rank, counts[0], tm)
        xg = _dispatch(hp, dest, zero_tiles, p=p_rows, tt=_pick(n, 128), tm=tm)
        ff = w2.shape[2]
        act = _gmm1(xg, w1, b1, l, tile_e, n_used, tile_rows, tm=tm, tf=_pick(ff, 512))
        tn = _pick(d, 2048)
        y = _gmm2(act, w2, b2, l, tile_e, n_used, tile_rows, tm=tm, tn=tn)
        xf = _combine(xf, y, dest, gate, g_final, tt=_pick(n, 64), pack_cols=tn,
                      final_norm=l == depth - 1)
    return xf.reshape(batch, seq, d)
```

```python
import functools
import math

import jax
import jax.numpy as jnp
from jax import lax
from jax.experimental import pallas as pl
from jax.experimental.pallas import tpu as pltpu

F32 = jnp.float32
BF16 = jnp.bfloat16
U32 = jnp.uint32

DIFF_HEAD_DIM = 128
CONV_KERNEL = 31
N_BUCKETS = 32
MAX_DISTANCE = 128
N_CROSS_HEADS = 4
TOP_K = 4
SWIGLU_LIMIT = 7.0
SWIGLU_ALPHA = 1.702
EPS = 1e-6
LOG2_E = math.log2(math.e)
N_BIAS_TILES = 5

V7X_VMEM_BYTES = 64 * 1024 * 1024
VMEM_LIMIT_BYTES = V7X_VMEM_BYTES - 6 * 1024 * 1024
LANES = 128
COMBINE_SLOTS = 3
COMBINE_GROUP = 8
CONV_RBLOCK = 64
CONV_CBLOCK = 256
HALO_ROWS = 16

_T5_LOG_THRESHOLDS = tuple(
    math.ceil(8 * (MAX_DISTANCE / 8) ** (k / 8) - 1e-9) for k in range(1, 8))


def _params(*semantics):
    return pltpu.CompilerParams(dimension_semantics=semantics,
                                vmem_limit_bytes=VMEM_LIMIT_BYTES)


def _rmsnorm_kernel(x_ref, g_ref, o_ref):
    x = x_ref[...]
    y = x * lax.rsqrt(jnp.mean(x * x, axis=-1, keepdims=True) + EPS)
    o_ref[...] = (y * g_ref[...]).astype(o_ref.dtype)


def _rmsnorm(x, g, *, rows):
    n, d = x.shape
    return pl.pallas_call(
        _rmsnorm_kernel,
        out_shape=jax.ShapeDtypeStruct((n, d), BF16),
        grid=(n // rows,),
        in_specs=[pl.BlockSpec((rows, d), lambda i: (i, 0)),
                  pl.BlockSpec((1, d), lambda i: (0, 0))],
        out_specs=pl.BlockSpec((rows, d), lambda i: (i, 0)),
        compiler_params=_params("arbitrary"),
        name="rmsnorm",
    )(x, g.reshape(1, d))


def _dense_kernel(*refs, k_splits, has_norm, has_res):
    n_lhs = len(k_splits)
    x_refs = refs[:n_lhs]
    rest = list(refs[n_lhs:-2])
    g_ref = rest.pop(0) if has_norm else None
    w_ref = rest.pop(0)
    res_ref = rest.pop(0) if has_res else None
    o_ref, wbf_ref = refs[-2], refs[-1]

    @pl.when(pl.program_id(1) == 0)
    def _():
        wbf_ref[...] = w_ref[...].astype(BF16)

    acc = None
    k0 = 0
    for x_ref, kw in zip(x_refs, k_splits):
        x = x_ref[...]
        if has_norm:
            x = x * lax.rsqrt(jnp.mean(x * x, axis=-1, keepdims=True) + EPS)
            x = (x * g_ref[...]).astype(BF16)
        part = jnp.dot(x, wbf_ref[k0:k0 + kw, :], preferred_element_type=F32)
        acc = part if acc is None else acc + part
        k0 += kw
    if has_res:
        acc = res_ref[...] + acc
    o_ref[...] = acc.astype(o_ref.dtype)


def _dense(xs, w, layer, *, tm, tn, out_dtype, norm_g=None, res=None, name):
    m = xs[0].shape[0]
    k_splits = tuple(x.shape[1] for x in xs)
    k, n = w.shape[1], w.shape[2]
    assert sum(k_splits) == k and m % tm == 0 and n % tn == 0
    assert norm_g is None or len(xs) == 1
    in_specs = [pl.BlockSpec((tm, kw), lambda j, i: (i, 0)) for kw in k_splits]
    args = list(xs)
    if norm_g is not None:
        in_specs.append(pl.BlockSpec((1, k), lambda j, i: (0, 0)))
        args.append(norm_g.reshape(1, k))
    in_specs.append(pl.BlockSpec((None, k, tn), lambda j, i: (layer, 0, j)))
    args.append(w)
    if res is not None:
        in_specs.append(pl.BlockSpec((tm, tn), lambda j, i: (i, j)))
        args.append(res)
    return pl.pallas_call(
        functools.partial(_dense_kernel, k_splits=k_splits, has_norm=norm_g is not None,
                          has_res=res is not None),
        out_shape=jax.ShapeDtypeStruct((m, n), out_dtype),
        grid=(n // tn, m // tm),
        in_specs=in_specs,
        out_specs=pl.BlockSpec((tm, tn), lambda j, i: (i, j)),
        scratch_shapes=[pltpu.VMEM((k, tn), BF16)],
        compiler_params=_params("arbitrary", "arbitrary"),
        name=name,
    )(*args)


def _t5_bucket(rel):
    half = N_BUCKETS // 2
    max_exact = half // 2
    n = jnp.abs(rel)
    large = jnp.full(rel.shape, max_exact, jnp.int32)
    for thr in _T5_LOG_THRESHOLDS:
        large = large + (n >= thr).astype(jnp.int32)
    return jnp.where(rel > 0, half, 0) + jnp.where(n < max_exact, n, large)


def _t5_bucket_static(rel):
    half = N_BUCKETS // 2
    n = abs(rel)
    large = half // 2 + sum(n >= thr for thr in _T5_LOG_THRESHOLDS)
    return (half if rel > 0 else 0) + (n if n < half // 2 else large)


def _attn_prep_kernel(rb_ref, lq1_ref, lk1_ref, lq2_ref, lk2_ref, tbl_ref, lam_ref, *,
                      tb, lambda_init):
    h = pl.program_id(0)
    a = lax.broadcasted_iota(jnp.int32, (tb, tb), 0)
    b = lax.broadcasted_iota(jnp.int32, (tb, tb), 1)
    for u in range(N_BIAS_TILES):
        origin = (u - N_BIAS_TILES // 2) * tb
        reachable = sorted({_t5_bucket_static(rel)
                            for rel in range(origin - tb + 1, origin + tb)})
        bucket = _t5_bucket(origin + b - a)
        val = jnp.full((tb, tb), rb_ref[reachable[0], h], F32)
        for bk in reachable[1:]:
            val = jnp.where(bucket == bk, rb_ref[bk, h], val)
        tbl_ref[0, u] = val * LOG2_E
    s1 = jnp.sum(lq1_ref[...] * lk1_ref[...], axis=-1, keepdims=True)
    s2 = jnp.sum(lq2_ref[...] * lk2_ref[...], axis=-1, keepdims=True)
    lam_ref[...] = jnp.exp(s1) - jnp.exp(s2) + lambda_init


def _attn_prep(rel_bias, lq1, lk1, lq2, lk2, *, tb, lambda_init):
    n_heads = rel_bias.shape[1]
    assert tb > MAX_DISTANCE
    vec = pl.BlockSpec((1, DIFF_HEAD_DIM), lambda h: (0, 0))
    return pl.pallas_call(
        functools.partial(_attn_prep_kernel, tb=tb, lambda_init=lambda_init),
        out_shape=(jax.ShapeDtypeStruct((n_heads, N_BIAS_TILES, tb, tb), F32),
                   jax.ShapeDtypeStruct((1, 1), F32)),
        grid=(n_heads,),
        in_specs=[pl.BlockSpec(memory_space=pltpu.SMEM), vec, vec, vec, vec],
        out_specs=(pl.BlockSpec((1, N_BIAS_TILES, tb, tb), lambda h: (h, 0, 0, 0)),
                   pl.BlockSpec((1, 1), lambda h: (0, 0))),
        compiler_params=_params("arbitrary"),
        name="attn_prep",
    )(rel_bias, lq1.reshape(1, -1), lk1.reshape(1, -1), lq2.reshape(1, -1), lk2.reshape(1, -1))


def _diff_attn_kernel(lam_ref, q_ref, k_ref, v_ref, tbl_ref, g_ref, o_ref, *,
                      tq, tk, seq, lambda_init):
    i = pl.program_id(2)
    dh = DIFF_HEAD_DIM
    scale = dh ** -0.5 * LOG2_E
    q = q_ref[0]
    first_half = lax.broadcasted_iota(jnp.int32, q.shape, 1) < dh
    zero = jnp.zeros(q.shape, q.dtype)
    qq = jnp.concatenate([jnp.where(first_half, q, zero), jnp.where(first_half, zero, q)], axis=0)
    tiles_per_chunk = tk // tq
    parts = []
    for c in range(seq // tk):
        keys = slice(c * tk, (c + 1) * tk)
        s = lax.dot_general(qq, k_ref[0, keys, :], (((1,), (1,)), ((), ())),
                            preferred_element_type=F32)
        bias = jnp.concatenate(
            [tbl_ref[0, jnp.clip(c * tiles_per_chunk + j - i + N_BIAS_TILES // 2,
                                 0, N_BIAS_TILES - 1)] for j in range(tiles_per_chunk)], axis=1)
        t = s * scale + jnp.concatenate([bias, bias], axis=0)
        m = jnp.max(t, axis=-1, keepdims=True)
        e = jnp.exp2(t - m)
        parts.append((m, jnp.sum(e, axis=-1, keepdims=True),
                      jnp.dot(e.astype(BF16), v_ref[0, keys, :], preferred_element_type=F32)))
    m_all = parts[0][0]
    for m, _, _ in parts[1:]:
        m_all = jnp.maximum(m_all, m)
    l_all = jnp.zeros_like(m_all)
    o = jnp.zeros(parts[0][2].shape, F32)
    for m, l, oc in parts:
        w = jnp.exp2(m - m_all)
        l_all = l_all + w * l
        o = o + w * oc
    o = o * (1.0 / l_all)
    o = o[:tq] - lam_ref[0, 0] * o[tq:]
    y = o * lax.rsqrt(jnp.mean(o * o, axis=-1, keepdims=True) + EPS)
    o_ref[0] = ((y * g_ref[...]) * (1.0 - lambda_init)).astype(o_ref.dtype)


def _diff_attention(u, tbl, lam, g_subln, *, batch, seq, n_heads, tq, tk, lambda_init):
    hw = 2 * DIFF_HEAD_DIM
    assert tbl.shape[2] == tq and tk % tq == 0 and seq % tk == 0
    return pl.pallas_call(
        functools.partial(_diff_attn_kernel, tq=tq, tk=tk, seq=seq, lambda_init=lambda_init),
        out_shape=jax.ShapeDtypeStruct((batch, seq, n_heads * hw), BF16),
        grid=(batch, n_heads, seq // tq),
        in_specs=[
            pl.BlockSpec(memory_space=pltpu.SMEM),
            pl.BlockSpec((1, tq, hw), lambda b, h, i: (b, i, h)),
            pl.BlockSpec((1, seq, hw), lambda b, h, i: (b, 0, n_heads + h)),
            pl.BlockSpec((1, seq, hw), lambda b, h, i: (b, 0, 2 * n_heads + h)),
            pl.BlockSpec((1, N_BIAS_TILES, tq, tq), lambda b, h, i: (h, 0, 0, 0)),
            pl.BlockSpec((1, hw), lambda b, h, i: (0, 0)),
        ],
        out_specs=pl.BlockSpec((1, tq, hw), lambda b, h, i: (b, i, h)),
        compiler_params=_params("arbitrary", "arbitrary", "arbitrary"),
        name="diff_attention",
    )(lam, u, u, u, tbl, g_subln.reshape(1, hw))


def _conv_kernel(a_ref, ap_ref, an_ref, g_ref, gp_ref, gn_ref, w_ref, b_ref, lg_ref, lb_ref,
                 o_ref, z_ref, zs_ref, y_ref, *, ts):
    i = pl.program_id(1)
    last = pl.num_programs(1) - 1

    def glu(a, g):
        return a[0].astype(F32) * jax.nn.sigmoid(g[0].astype(F32))

    z_ref[:HALO_ROWS, :] = glu(ap_ref, gp_ref) * (i > 0).astype(F32)
    z_ref[HALO_ROWS:HALO_ROWS + ts, :] = glu(a_ref, g_ref)
    z_ref[HALO_ROWS + ts:, :] = glu(an_ref, gn_ref) * (i < last).astype(F32)
    first = HALO_ROWS - CONV_KERNEL // 2
    span = zs_ref.shape[1]
    for res in range(1, 8):
        zs_ref[res - 1] = z_ref[res:res + span, :]
    n_cblocks = y_ref.shape[1] // CONV_CBLOCK
    for r0 in range(0, ts, CONV_RBLOCK):
        def channel_block(cb, carry, r0=r0):
            cols = pl.ds(pl.multiple_of(cb * CONV_CBLOCK, CONV_CBLOCK), CONV_CBLOCK)
            acc = jnp.zeros((CONV_RBLOCK, CONV_CBLOCK), F32)
            for t in range(CONV_KERNEL):
                res, base = (first + t) % 8, (first + t) // 8 * 8 + r0
                rows = slice(base, base + CONV_RBLOCK)
                shifted = z_ref[rows, cols] if res == 0 else zs_ref[res - 1, rows, cols]
                acc = acc + w_ref[t:t + 1, cols] * shifted
            y_ref[r0:r0 + CONV_RBLOCK, cols] = acc + b_ref[:, cols]
            return carry
        lax.fori_loop(0, n_cblocks, channel_block, 0)
    y = y_ref[...]
    yc = y - jnp.mean(y, axis=-1, keepdims=True)
    yn = yc * lax.rsqrt(jnp.mean(yc * yc, axis=-1, keepdims=True) + EPS)
    yn = yn * lg_ref[...] + lb_ref[...]
    o_ref[0] = (yn * jax.nn.sigmoid(yn)).astype(o_ref.dtype)


def _conv_module(u, w_dw, b_dw, g_ln, b_ln, *, batch, seq, col_block, ts):
    cw = w_dw.shape[1]
    hb = ts // HALO_ROWS
    n_hblocks = seq // HALO_ROWS

    def main(c):
        return pl.BlockSpec((1, ts, cw), lambda b, i: (b, i, c))

    def prev(c):
        return pl.BlockSpec((1, HALO_ROWS, cw), lambda b, i: (b, jnp.maximum(i * hb - 1, 0), c))

    def nxt(c):
        return pl.BlockSpec((1, HALO_ROWS, cw),
                            lambda b, i: (b, jnp.minimum((i + 1) * hb, n_hblocks - 1), c))

    row = pl.BlockSpec((1, cw), lambda b, i: (0, 0))
    return pl.pallas_call(
        functools.partial(_conv_kernel, ts=ts),
        out_shape=jax.ShapeDtypeStruct((batch, seq, cw), BF16),
        grid=(batch, seq // ts),
        in_specs=[main(col_block), prev(col_block), nxt(col_block),
                  main(col_block + 1), prev(col_block + 1), nxt(col_block + 1),
                  pl.BlockSpec((CONV_KERNEL, cw), lambda b, i: (0, 0)), row, row, row],
        out_specs=pl.BlockSpec((1, ts, cw), lambda b, i: (b, i, 0)),
        scratch_shapes=[pltpu.VMEM((ts + 2 * HALO_ROWS, cw), F32),
                        pltpu.VMEM((7, ts + 2 * HALO_ROWS - 8, cw), F32),
                        pltpu.VMEM((ts, cw), F32)],
        compiler_params=_params("arbitrary", "arbitrary"),
        name="conv_module",
    )(u, u, u, u, u, u, w_dw, b_dw.reshape(1, cw), g_ln.reshape(1, cw), b_ln.reshape(1, cw))


def _cross_attn_kernel(q_ref, k_ref, v_ref, o_ref, *, head_dim):
    scale = head_dim ** -0.5
    for h in range(N_CROSS_HEADS):
        cols = slice(h * head_dim, (h + 1) * head_dim)
        s = lax.dot_general(q_ref[:, cols], k_ref[:, cols], (((1,), (1,)), ((), ())),
                            preferred_element_type=F32) * scale
        e = jnp.exp(s - jnp.max(s, axis=-1, keepdims=True))
        p = e * (1.0 / jnp.sum(e, axis=-1, keepdims=True))
        o_ref[:, cols] = jnp.dot(p.astype(BF16), v_ref[:, cols],
                                 preferred_element_type=F32).astype(o_ref.dtype)


def _cross_attention(q, k, v, *, batch, seq, n_mem, tm):
    width = q.shape[1]
    qb = seq // tm
    return pl.pallas_call(
        functools.partial(_cross_attn_kernel, head_dim=width // N_CROSS_HEADS),
        out_shape=jax.ShapeDtypeStruct(q.shape, BF16),
        grid=(batch, qb),
        in_specs=[pl.BlockSpec((tm, width), lambda b, i: (b * qb + i, 0)),
                  pl.BlockSpec((n_mem, width), lambda b, i: (b, 0)),
                  pl.BlockSpec((n_mem, width), lambda b, i: (b, 0))],
        out_specs=pl.BlockSpec((tm, width), lambda b, i: (b * qb + i, 0)),
        compiler_params=_params("arbitrary", "arbitrary"),
        name="cross_attention",
    )(q, k, v)


def _pack_bf16_pairs(lo_half, hi_half):
    a = pltpu.bitcast(lo_half, U32)
    b = pltpu.bitcast(hi_half, U32)
    return (a & jnp.uint32(0xFFFF0000)) | (b >> 16)


def _unpack_bf16_pairs(words):
    a = pltpu.bitcast(words & jnp.uint32(0xFFFF0000), F32)
    b = pltpu.bitcast(words << 16, F32)
    return a.astype(BF16), b.astype(BF16)


def _router_kernel(x_ref, g_ref, w_ref, b_ref, hp_ref, idx_ref, gate_ref, rank_ref, cnt_ref,
                   seen_ref):
    @pl.when(pl.program_id(0) == 0)
    def _():
        seen_ref[...] = jnp.zeros(seen_ref.shape, F32)

    x = x_ref[...]
    tm, d = x.shape
    y = x * lax.rsqrt(jnp.mean(x * x, axis=-1, keepdims=True) + EPS)
    hb = (y * g_ref[...]).astype(BF16)
    hp_ref[...] = _pack_bf16_pairs(hb[:, :d // 2].astype(F32), hb[:, d // 2:].astype(F32))
    logits = jnp.dot(hb, w_ref[...].astype(BF16), preferred_element_type=F32) + b_ref[...]
    n_exp = logits.shape[1]
    lane = lax.broadcasted_iota(jnp.int32, logits.shape, 1)
    kl = lax.broadcasted_iota(jnp.int32, idx_ref.shape, 1)
    vals = logits
    top_v = jnp.zeros(gate_ref.shape, F32)
    top_i = jnp.zeros(idx_ref.shape, jnp.int32)
    picks = []
    for kk in range(TOP_K):
        mx = jnp.max(vals, axis=-1, keepdims=True)
        sel = jnp.min(jnp.where(vals == mx, lane, n_exp), axis=-1, keepdims=True)
        top_v = jnp.where(kl == kk, mx, top_v)
        top_i = jnp.where(kl == kk, sel, top_i)
        picks.append(lane == sel)
        vals = jnp.where(picks[-1], -jnp.inf, vals)
    e = jnp.exp(top_v - jnp.max(top_v, axis=-1, keepdims=True))
    gate_ref[...] = e / jnp.sum(e, axis=-1, keepdims=True)
    idx_ref[...] = top_i
    member = jnp.zeros(logits.shape, F32)
    for pick in picks:
        member = member + pick.astype(F32)
    earlier = (lax.broadcasted_iota(jnp.int32, (tm, tm), 0)
               > lax.broadcasted_iota(jnp.int32, (tm, tm), 1)).astype(BF16)
    before = seen_ref[...] + jnp.dot(earlier, member.astype(BF16), preferred_element_type=F32)
    rank = jnp.zeros(rank_ref.shape, F32)
    for kk, pick in enumerate(picks):
        rank = jnp.where(kl == kk, jnp.sum(jnp.where(pick, before, 0.0), axis=-1, keepdims=True),
                         rank)
    rank_ref[...] = rank.astype(jnp.int32)
    seen_ref[...] = seen_ref[...] + jnp.sum(member, axis=0, keepdims=True)
    cnt_ref[...] = seen_ref[...].astype(jnp.int32)


def _router(x, g, w_router, b_router, layer, *, tm):
    n, d = x.shape
    n_exp = w_router.shape[2]
    per_tok = pl.BlockSpec((tm, TOP_K), lambda i: (i, 0))
    return pl.pallas_call(
        _router_kernel,
        out_shape=(jax.ShapeDtypeStruct((n, d // 2), U32),
                   jax.ShapeDtypeStruct((n, TOP_K), jnp.int32),
                   jax.ShapeDtypeStruct((n, TOP_K), F32),
                   jax.ShapeDtypeStruct((n, TOP_K), jnp.int32),
                   jax.ShapeDtypeStruct((1, n_exp), jnp.int32)),
        grid=(n // tm,),
        in_specs=[pl.BlockSpec((tm, d), lambda i: (i, 0)),
                  pl.BlockSpec((1, d), lambda i: (0, 0)),
                  pl.BlockSpec((None, d, n_exp), lambda i: (layer, 0, 0)),
                  pl.BlockSpec((1, n_exp), lambda i: (0, 0))],
        out_specs=(pl.BlockSpec((tm, d // 2), lambda i: (i, 0)), per_tok, per_tok, per_tok,
                   pl.BlockSpec((1, n_exp), lambda i: (0, 0))),
        scratch_shapes=[pltpu.VMEM((1, n_exp), F32)],
        compiler_params=_params("arbitrary"),
        name="router",
    )(x, g.reshape(1, d), w_router, b_router.reshape(1, n_exp))


def _dispatch_kernel(zt_ref, dest_ref, src_ref, dst_ref, zero_ref, zsem, sem, *, tt, tm):
    @pl.when(pl.program_id(0) == 0)
    def _():
        zero_ref[...] = jnp.zeros(zero_ref.shape, zero_ref.dtype)

        def zero_copy(i):
            return pltpu.make_async_copy(zero_ref, dst_ref.at[pl.ds(zt_ref[i] * tm, tm)], zsem.at[0])

        def start(i, carry):
            @pl.when(zt_ref[i] >= 0)
            def _():
                zero_copy(i).start()
            return carry

        def wait(i, carry):
            @pl.when(zt_ref[i] >= 0)
            def _():
                zero_copy(i).wait()
            return carry

        lax.fori_loop(0, zt_ref.shape[0], start, 0)
        lax.fori_loop(0, zt_ref.shape[0], wait, 0)

    def row_copy(r, kk):
        return pltpu.make_async_copy(src_ref.at[pl.ds(r, 1)],
                                     dst_ref.at[pl.ds(dest_ref[0, 0, r * TOP_K + kk], 1)],
                                     sem.at[0])

    for r in range(tt):
        for kk in range(TOP_K):
            row_copy(r, kk).start(priority=kk % 2)
    for kk in range(TOP_K):
        pltpu.make_async_copy(src_ref, dst_ref.at[pl.ds(0, tt)], sem.at[0]).wait()


def _dispatch(src, dest, zero_tiles, *, p, tt, tm):
    n, half = src.shape
    nt = n // tt
    return pl.pallas_call(
        functools.partial(_dispatch_kernel, tt=tt, tm=tm),
        out_shape=jax.ShapeDtypeStruct((p, half), src.dtype),
        grid_spec=pltpu.PrefetchScalarGridSpec(
            num_scalar_prefetch=1,
            grid=(nt,),
            in_specs=[pl.BlockSpec((1, 1, tt * TOP_K), lambda t, zt: (t, 0, 0),
                                   memory_space=pltpu.SMEM),
                      pl.BlockSpec((tt, half), lambda t, zt: (t, 0))],
            out_specs=pl.BlockSpec(memory_space=pl.ANY),
            scratch_shapes=[pltpu.VMEM((tm, half), src.dtype),
                            pltpu.SemaphoreType.DMA((1,)), pltpu.SemaphoreType.DMA((1,))],
        ),
        compiler_params=_params("arbitrary"),
        name="moe_dispatch",
    )(zero_tiles, dest.reshape(nt, 1, tt * TOP_K), src)


def _is_first_tile_of_expert(te_ref, t):
    return jnp.logical_or(t == 0, te_ref[t] != te_ref[jnp.maximum(t - 1, 0)])


def _gmm1_kernel(te_ref, nu_ref, x_ref, wa_ref, wb_ref, ba_ref, bb_ref, o_ref, wa_bf, wb_bf):
    t = pl.program_id(1)

    @pl.when(t < nu_ref[0])
    def _():
        @pl.when(_is_first_tile_of_expert(te_ref, t))
        def _():
            wa_bf[...] = wa_ref[...].astype(BF16)
            wb_bf[...] = wb_ref[...].astype(BF16)

        x_lo, x_hi = _unpack_bf16_pairs(x_ref[...])
        half = x_lo.shape[1]

        def proj(w_bf, b_ref):
            return (jnp.dot(x_lo, w_bf[:half, :], preferred_element_type=F32)
                    + jnp.dot(x_hi, w_bf[half:, :], preferred_element_type=F32) + b_ref[...])

        a = jnp.minimum(proj(wa_bf, ba_ref), SWIGLU_LIMIT)
        b = jnp.clip(proj(wb_bf, bb_ref), -SWIGLU_LIMIT, SWIGLU_LIMIT)
        o_ref[...] = (a * jax.nn.sigmoid(SWIGLU_ALPHA * a) * (b + 1.0)).astype(o_ref.dtype)


def _gmm1(xg, w1, b1, layer, tile_e, n_used, *, tm, tf):
    p, half = xg.shape
    d = 2 * half
    n_exp, ff = w1.shape[1], w1.shape[3] // 2
    nj = ff // tf
    assert ff % tf == 0 and p % tm == 0

    def row(t, nu):
        return jnp.minimum(t, nu[0] - 1)

    return pl.pallas_call(
        _gmm1_kernel,
        out_shape=jax.ShapeDtypeStruct((p, ff), BF16),
        grid_spec=pltpu.PrefetchScalarGridSpec(
            num_scalar_prefetch=2,
            grid=(nj, p // tm),
            in_specs=[
                pl.BlockSpec((tm, half), lambda j, t, te, nu: (row(t, nu), 0)),
                pl.BlockSpec((None, None, d, tf), lambda j, t, te, nu: (layer, te[t], 0, j)),
                pl.BlockSpec((None, None, d, tf),
                             lambda j, t, te, nu: (layer, te[t], 0, nj + j)),
                pl.BlockSpec((None, None, 1, tf), lambda j, t, te, nu: (layer, te[t], 0, j)),
                pl.BlockSpec((None, None, 1, tf),
                             lambda j, t, te, nu: (layer, te[t], 0, nj + j)),
            ],
            out_specs=pl.BlockSpec((tm, tf), lambda j, t, te, nu: (row(t, nu), j)),
            scratch_shapes=[pltpu.VMEM((d, tf), BF16), pltpu.VMEM((d, tf), BF16)],
        ),
        compiler_params=_params("arbitrary", "arbitrary"),
        name="moe_gmm1",
    )(tile_e, n_used, xg, w1, w1, b1.reshape(b1.shape[0], n_exp, 1, 2 * ff),
      b1.reshape(b1.shape[0], n_exp, 1, 2 * ff))


def _gmm2_kernel(te_ref, nu_ref, x_ref, w_ref, b_ref, o_ref, w_bf):
    t = pl.program_id(1)

    @pl.when(t < nu_ref[0])
    def _():
        @pl.when(_is_first_tile_of_expert(te_ref, t))
        def _():
            w_bf[...] = w_ref[...].astype(BF16)

        y = jnp.dot(x_ref[...], w_bf[...], preferred_element_type=F32) + b_ref[...]
        yb = y.astype(BF16).astype(F32)
        half = yb.shape[1] // 2
        o_ref[...] = _pack_bf16_pairs(yb[:, :half], yb[:, half:])


def _gmm2(act, w2, b2, layer, tile_e, n_used, *, tm, tn):
    p, ff = act.shape
    n_exp, d = w2.shape[1], w2.shape[3]
    assert d % tn == 0

    def row(t, nu):
        return jnp.minimum(t, nu[0] - 1)

    return pl.pallas_call(
        _gmm2_kernel,
        out_shape=jax.ShapeDtypeStruct((p, d // 2), U32),
        grid_spec=pltpu.PrefetchScalarGridSpec(
            num_scalar_prefetch=2,
            grid=(d // tn, p // tm),
            in_specs=[
                pl.BlockSpec((tm, ff), lambda j, t, te, nu: (row(t, nu), 0)),
                pl.BlockSpec((None, None, ff, tn), lambda j, t, te, nu: (layer, te[t], 0, j)),
                pl.BlockSpec((None, None, 1, tn), lambda j, t, te, nu: (layer, te[t], 0, j)),
            ],
            out_specs=pl.BlockSpec((tm, tn // 2), lambda j, t, te, nu: (row(t, nu), j)),
            scratch_shapes=[pltpu.VMEM((ff, tn), BF16)],
        ),
        compiler_params=_params("arbitrary", "arbitrary"),
        name="moe_gmm2",
    )(tile_e, n_used, act, w2, b2.reshape(b2.shape[0], n_exp, 1, d))


def _unpack_expert_rows(words, pack_cols):
    lo = pltpu.bitcast(words & jnp.uint32(0xFFFF0000), F32)
    hi = pltpu.bitcast(words << 16, F32)
    half = pack_cols // 2
    pieces = []
    for c in range(0, words.shape[1], half):
        pieces += [lo[:, c:c + half], hi[:, c:c + half]]
    return jnp.concatenate(pieces, axis=1)


def _combine_kernel(d0_ref, d1_ref, d2_ref, x_ref, gate_ref, g_ref, y_ref, o_ref, buf, sem, *,
                    tt, pack_cols, final_norm):
    t = pl.program_id(0)
    last = pl.num_programs(0) - 1
    slot = lax.rem(t, COMBINE_SLOTS)
    ahead = lax.rem(t + 2, COMBINE_SLOTS)

    def row_copy(d_ref, r, kk, s):
        return pltpu.make_async_copy(y_ref.at[pl.ds(d_ref[0, 0, r * TOP_K + kk], 1)],
                                     buf.at[s, kk, pl.ds(r, 1)], sem.at[s])

    def issue(d_ref, s, rows):
        for r in rows:
            for kk in range(TOP_K):
                row_copy(d_ref, r, kk, s).start(priority=kk % 2)

    def drain(s):
        for kk in range(TOP_K):
            pltpu.make_async_copy(y_ref.at[pl.ds(0, tt)], buf.at[s, kk], sem.at[s]).wait()

    @pl.when(t == 0)
    def _():
        issue(d0_ref, 0, range(tt))
        issue(d1_ref, 1, range(tt))

    drain(slot)
    gate = gate_ref[...]
    for r0 in range(0, tt, COMBINE_GROUP):
        rows = slice(r0, r0 + COMBINE_GROUP)
        acc = x_ref[rows, :]
        parts = [_unpack_expert_rows(buf[slot, kk, rows, :], pack_cols) for kk in range(TOP_K)]
        issue(d2_ref, ahead, range(r0, r0 + COMBINE_GROUP))
        for kk in range(TOP_K):
            acc = acc + gate[rows, kk:kk + 1] * parts[kk]
        if final_norm:
            acc = acc * lax.rsqrt(jnp.mean(acc * acc, axis=-1, keepdims=True) + EPS) * g_ref[...]
        o_ref[rows, :] = acc

    @pl.when(t == last)
    def _():
        drain(lax.rem(t + 1, COMBINE_SLOTS))
        drain(ahead)


def _combine(x, y, dest, gate, g_final, *, tt, pack_cols, final_norm):
    n, d = x.shape
    nt = n // tt
    assert nt >= COMBINE_SLOTS and tt % COMBINE_GROUP == 0
    dest3 = dest.reshape(nt, 1, tt * TOP_K)

    def tile(k):
        return pl.BlockSpec((1, 1, tt * TOP_K), lambda t: (jnp.minimum(t + k, nt - 1), 0, 0),
                            memory_space=pltpu.SMEM)

    return pl.pallas_call(
        functools.partial(_combine_kernel, tt=tt, pack_cols=pack_cols, final_norm=final_norm),
        out_shape=jax.ShapeDtypeStruct((n, d), F32),
        grid=(nt,),
        in_specs=[
            tile(0), tile(1), tile(2),
            pl.BlockSpec((tt, d), lambda t: (t, 0)),
            pl.BlockSpec((tt, TOP_K), lambda t: (t, 0)),
            pl.BlockSpec((1, d), lambda t: (0, 0)),
            pl.BlockSpec(memory_space=pl.ANY),
        ],
        out_specs=pl.BlockSpec((tt, d), lambda t: (t, 0)),
        scratch_shapes=[pltpu.VMEM((COMBINE_SLOTS, TOP_K, tt, d // 2), U32),
                        pltpu.SemaphoreType.DMA((COMBINE_SLOTS,))],
        compiler_params=_params("arbitrary"),
        name="moe_combine",
    )(dest3, dest3, dest3, x, gate, g_final.reshape(1, d), y)


def _routing_tables(top_idx, rank, counts, tm):
    n = top_idx.shape[0]
    n_exp = counts.shape[0]
    n_tiles = (n * TOP_K) // tm + n_exp
    padded = (counts + tm - 1) // tm * tm
    pad_end = jnp.cumsum(padded)
    pad_start = pad_end - padded
    experts = jnp.arange(n_exp, dtype=jnp.int32)
    dest = rank + jnp.sum(jnp.where(top_idx[..., None] == experts, pad_start, 0), axis=-1)
    tile_start = jnp.arange(n_tiles, dtype=jnp.int32) * tm
    tile_e = jnp.minimum(jnp.sum(tile_start[:, None] >= pad_end[None, :], axis=-1),
                         n_exp - 1).astype(jnp.int32)
    n_used = pad_end[-1:] // tm
    ragged = jnp.where(counts % tm != 0, pad_end // tm - 1, -1)
    tail = n_used + experts
    zero_tiles = jnp.concatenate([ragged, jnp.where(tail < n_tiles, tail, -1)])
    return (dest.astype(jnp.int32), tile_e, n_used.astype(jnp.int32),
            zero_tiles.astype(jnp.int32), n_tiles * tm)


def _pick(n, pref):
    t = min(n, pref)
    while n % t or t % 8:
        t -= 1
    return t


def kernel(x, mem, g_mix, w_in, lambda_q1, lambda_k1, lambda_q2, lambda_k2, g_subln, w_dw, b_dw,
           g_conv_ln, b_conv_ln, w_out, rel_bias, g_cross, g_mem, w_cq, w_ck, w_cv, w_co, g_ffn,
           w_router, b_router, w1, b1, w2, b2, g_final):
    batch, seq, d = x.shape
    n = batch * seq
    n_mem = mem.shape[1]
    depth = g_mix.shape[0]
    n_heads = rel_bias.shape[1]
    attn_w = n_heads * 2 * DIFF_HEAD_DIM
    conv_w = w_dw.shape[2]
    in_w = w_in.shape[2]
    n_exp = w_router.shape[2]
    assert in_w == 3 * attn_w + 2 * conv_w and (3 * attn_w) % conv_w == 0

    tq = _pick(seq, 512)
    xf = x.reshape(n, d)
    memf = mem.reshape(batch * n_mem, d)
    for l in range(depth):
        lambda_init = 0.8 - 0.6 * math.exp(-0.3 * l)
        tbl, lam = _attn_prep(rel_bias, lambda_q1[l], lambda_k1[l], lambda_q2[l], lambda_k2[l],
                              tb=tq, lambda_init=lambda_init)
        h = _rmsnorm(xf, g_mix[l], rows=_pick(n, 256))
        u = _dense([h], w_in, l, tm=_pick(n, 512), tn=_pick(in_w, 1024), out_dtype=BF16,
                   name="in_proj").reshape(batch, seq, in_w)
        attn = _diff_attention(u, tbl, lam, g_subln[l], batch=batch, seq=seq, n_heads=n_heads,
                               tq=tq, tk=_pick(seq, 512), lambda_init=lambda_init)
        conv = _conv_module(u, w_dw[l], b_dw[l], g_conv_ln[l], b_conv_ln[l], batch=batch, seq=seq,
                            col_block=3 * attn_w // conv_w, ts=_pick(seq, 256))
        xf = _dense([attn.reshape(n, attn_w), conv.reshape(n, conv_w)], w_out, l,
                    tm=_pick(n, 1024), tn=_pick(d, 512), out_dtype=F32, res=xf, name="out_proj")
        m = _rmsnorm(memf, g_mem[l], rows=_pick(batch * n_mem, 256))
        cross_w = w_cq.shape[2]
        qc = _dense([xf], w_cq, l, tm=_pick(n, 256), tn=_pick(cross_w, 1024), out_dtype=BF16,
                    norm_g=g_cross[l], name="cross_q")
        kc = _dense([m], w_ck, l, tm=_pick(batch * n_mem, 512), tn=_pick(cross_w, 512),
                    out_dtype=BF16, name="cross_k")
        vc = _dense([m], w_cv, l, tm=_pick(batch * n_mem, 512), tn=_pick(cross_w, 512),
                    out_dtype=BF16, name="cross_v")
        oc = _cross_attention(qc, kc, vc, batch=batch, seq=seq, n_mem=n_mem, tm=_pick(seq, 512))
        xf = _dense([oc], w_co, l, tm=_pick(n, 512), tn=_pick(d, 2048), out_dtype=F32, res=xf,
                    name="cross_o")
        tm = 512
        hp, top_idx, gate, rank, counts = _router(xf, g_ffn[l], w_router, b_router[l], l,
                                                  tm=_pick(n, 256))
        dest, tile_e, n_used, zero_tiles, p_rows = _routing_tables(top_idx, rank, counts[0], tm)
        xg = _dispatch(hp, dest, zero_tiles, p=p_rows, tt=_pick(n, 128), tm=tm)
        ff = w2.shape[2]
        act = _gmm1(xg, w1, b1, l, tile_e, n_used, tm=tm, tf=_pick(ff, 512))
        tn = _pick(d, 2048)
        y = _gmm2(act, w2, b2, l, tile_e, n_used, tm=tm, tn=tn)
        xf = _combine(xf, y, dest, gate, g_final, tt=_pick(n, 128), pack_cols=tn,
                      final_norm=l == depth - 1)
    return xf.reshape(batch, seq, d)
```

```python
import functools
import math

import jax
import jax.numpy as jnp
from jax import lax
from jax.experimental import pallas as pl
from jax.experimental.pallas import tpu as pltpu

F32 = jnp.float32
BF16 = jnp.bfloat16
U32 = jnp.uint32

DIFF_HEAD_DIM = 128
CONV_KERNEL = 31
N_BUCKETS = 32
MAX_DISTANCE = 128
N_CROSS_HEADS = 4
TOP_K = 4
SWIGLU_LIMIT = 7.0
SWIGLU_ALPHA = 1.702
EPS = 1e-6
LOG2_E = math.log2(math.e)
N_BIAS_TILES = 5

V7X_VMEM_BYTES = 64 * 1024 * 1024
VMEM_LIMIT_BYTES = V7X_VMEM_BYTES - 6 * 1024 * 1024
LANES = 128
COMBINE_SLOTS = 3
COMBINE_GROUP = 8
CONV_RBLOCK = 64
CONV_CBLOCK = 256
HALO_ROWS = 16

_T5_LOG_THRESHOLDS = tuple(
    math.ceil(8 * (MAX_DISTANCE / 8) ** (k / 8) - 1e-9) for k in range(1, 8))


def _params(*semantics):
    return pltpu.CompilerParams(dimension_semantics=semantics,
                                vmem_limit_bytes=VMEM_LIMIT_BYTES)


def _rmsnorm_kernel(x_ref, g_ref, o_ref):
    x = x_ref[...]
    y = x * lax.rsqrt(jnp.mean(x * x, axis=-1, keepdims=True) + EPS)
    o_ref[...] = (y * g_ref[...]).astype(o_ref.dtype)


def _rmsnorm(x, g, *, rows):
    n, d = x.shape
    return pl.pallas_call(
        _rmsnorm_kernel,
        out_shape=jax.ShapeDtypeStruct((n, d), BF16),
        grid=(n // rows,),
        in_specs=[pl.BlockSpec((rows, d), lambda i: (i, 0)),
                  pl.BlockSpec((1, d), lambda i: (0, 0))],
        out_specs=pl.BlockSpec((rows, d), lambda i: (i, 0)),
        compiler_params=_params("arbitrary"),
        name="rmsnorm",
    )(x, g.reshape(1, d))


def _dense_kernel(*refs, k_splits, has_norm, has_res):
    n_lhs = len(k_splits)
    x_refs = refs[:n_lhs]
    rest = list(refs[n_lhs:-2])
    g_ref = rest.pop(0) if has_norm else None
    w_ref = rest.pop(0)
    res_ref = rest.pop(0) if has_res else None
    o_ref, wbf_ref = refs[-2], refs[-1]

    @pl.when(pl.program_id(1) == 0)
    def _():
        wbf_ref[...] = w_ref[...].astype(BF16)

    acc = None
    k0 = 0
    for x_ref, kw in zip(x_refs, k_splits):
        x = x_ref[...]
        if has_norm:
            x = x * lax.rsqrt(jnp.mean(x * x, axis=-1, keepdims=True) + EPS)
            x = (x * g_ref[...]).astype(BF16)
        part = jnp.dot(x, wbf_ref[k0:k0 + kw, :], preferred_element_type=F32)
        acc = part if acc is None else acc + part
        k0 += kw
    if has_res:
        acc = res_ref[...] + acc
    o_ref[...] = acc.astype(o_ref.dtype)


def _dense(xs, w, layer, *, tm, tn, out_dtype, norm_g=None, res=None, name):
    m = xs[0].shape[0]
    k_splits = tuple(x.shape[1] for x in xs)
    k, n = w.shape[1], w.shape[2]
    assert sum(k_splits) == k and m % tm == 0 and n % tn == 0
    assert norm_g is None or len(xs) == 1
    in_specs = [pl.BlockSpec((tm, kw), lambda j, i: (i, 0)) for kw in k_splits]
    args = list(xs)
    if norm_g is not None:
        in_specs.append(pl.BlockSpec((1, k), lambda j, i: (0, 0)))
        args.append(norm_g.reshape(1, k))
    in_specs.append(pl.BlockSpec((None, k, tn), lambda j, i: (layer, 0, j)))
    args.append(w)
    if res is not None:
        in_specs.append(pl.BlockSpec((tm, tn), lambda j, i: (i, j)))
        args.append(res)
    return pl.pallas_call(
        functools.partial(_dense_kernel, k_splits=k_splits, has_norm=norm_g is not None,
                          has_res=res is not None),
        out_shape=jax.ShapeDtypeStruct((m, n), out_dtype),
        grid=(n // tn, m // tm),
        in_specs=in_specs,
        out_specs=pl.BlockSpec((tm, tn), lambda j, i: (i, j)),
        scratch_shapes=[pltpu.VMEM((k, tn), BF16)],
        compiler_params=_params("arbitrary", "arbitrary"),
        name=name,
    )(*args)


def _t5_bucket(rel):
    half = N_BUCKETS // 2
    max_exact = half // 2
    n = jnp.abs(rel)
    large = jnp.full(rel.shape, max_exact, jnp.int32)
    for thr in _T5_LOG_THRESHOLDS:
        large = large + (n >= thr).astype(jnp.int32)
    return jnp.where(rel > 0, half, 0) + jnp.where(n < max_exact, n, large)


def _t5_bucket_static(rel):
    half = N_BUCKETS // 2
    n = abs(rel)
    large = half // 2 + sum(n >= thr for thr in _T5_LOG_THRESHOLDS)
    return (half if rel > 0 else 0) + (n if n < half // 2 else large)


def _attn_prep_kernel(rb_ref, lq1_ref, lk1_ref, lq2_ref, lk2_ref, tbl_ref, lam_ref, *,
                      tb, lambda_init):
    h = pl.program_id(0)
    a = lax.broadcasted_iota(jnp.int32, (tb, tb), 0)
    b = lax.broadcasted_iota(jnp.int32, (tb, tb), 1)
    for u in range(N_BIAS_TILES):
        origin = (u - N_BIAS_TILES // 2) * tb
        reachable = sorted({_t5_bucket_static(rel)
                            for rel in range(origin - tb + 1, origin + tb)})
        bucket = _t5_bucket(origin + b - a)
        val = jnp.full((tb, tb), rb_ref[reachable[0], h], F32)
        for bk in reachable[1:]:
            val = jnp.where(bucket == bk, rb_ref[bk, h], val)
        tbl_ref[0, u] = val * LOG2_E
    s1 = jnp.sum(lq1_ref[...] * lk1_ref[...], axis=-1, keepdims=True)
    s2 = jnp.sum(lq2_ref[...] * lk2_ref[...], axis=-1, keepdims=True)
    lam_ref[...] = jnp.exp(s1) - jnp.exp(s2) + lambda_init


def _attn_prep(rel_bias, lq1, lk1, lq2, lk2, *, tb, lambda_init):
    n_heads = rel_bias.shape[1]
    assert tb > MAX_DISTANCE
    vec = pl.BlockSpec((1, DIFF_HEAD_DIM), lambda h: (0, 0))
    return pl.pallas_call(
        functools.partial(_attn_prep_kernel, tb=tb, lambda_init=lambda_init),
        out_shape=(jax.ShapeDtypeStruct((n_heads, N_BIAS_TILES, tb, tb), F32),
                   jax.ShapeDtypeStruct((1, 1), F32)),
        grid=(n_heads,),
        in_specs=[pl.BlockSpec(memory_space=pltpu.SMEM), vec, vec, vec, vec],
        out_specs=(pl.BlockSpec((1, N_BIAS_TILES, tb, tb), lambda h: (h, 0, 0, 0)),
                   pl.BlockSpec((1, 1), lambda h: (0, 0))),
        compiler_params=_params("arbitrary"),
        name="attn_prep",
    )(rel_bias, lq1.reshape(1, -1), lk1.reshape(1, -1), lq2.reshape(1, -1), lk2.reshape(1, -1))


def _diff_attn_kernel(lam_ref, q_ref, k_ref, v_ref, tbl_ref, g_ref, o_ref, *,
                      tq, tk, seq, lambda_init):
    i = pl.program_id(2)
    dh = DIFF_HEAD_DIM
    scale = dh ** -0.5 * LOG2_E
    q = q_ref[0]
    first_half = lax.broadcasted_iota(jnp.int32, q.shape, 1) < dh
    zero = jnp.zeros(q.shape, q.dtype)
    qq = jnp.concatenate([jnp.where(first_half, q, zero), jnp.where(first_half, zero, q)], axis=0)
    tiles_per_chunk = tk // tq
    parts = []
    for c in range(seq // tk):
        keys = slice(c * tk, (c + 1) * tk)
        s = lax.dot_general(qq, k_ref[0, keys, :], (((1,), (1,)), ((), ())),
                            preferred_element_type=F32)
        bias = jnp.concatenate(
            [tbl_ref[0, jnp.clip(c * tiles_per_chunk + j - i + N_BIAS_TILES // 2,
                                 0, N_BIAS_TILES - 1)] for j in range(tiles_per_chunk)], axis=1)
        t = s * scale + jnp.concatenate([bias, bias], axis=0)
        m = jnp.max(t, axis=-1, keepdims=True)
        e = jnp.exp2(t - m)
        parts.append((m, jnp.sum(e, axis=-1, keepdims=True),
                      jnp.dot(e.astype(BF16), v_ref[0, keys, :], preferred_element_type=F32)))
    m_all = parts[0][0]
    for m, _, _ in parts[1:]:
        m_all = jnp.maximum(m_all, m)
    l_all = jnp.zeros_like(m_all)
    o = jnp.zeros(parts[0][2].shape, F32)
    for m, l, oc in parts:
        w = jnp.exp2(m - m_all)
        l_all = l_all + w * l
        o = o + w * oc
    o = o * (1.0 / l_all)
    o = o[:tq] - lam_ref[0, 0] * o[tq:]
    y = o * lax.rsqrt(jnp.mean(o * o, axis=-1, keepdims=True) + EPS)
    o_ref[0] = ((y * g_ref[...]) * (1.0 - lambda_init)).astype(o_ref.dtype)


def _diff_attention(u, tbl, lam, g_subln, *, batch, seq, n_heads, tq, tk, lambda_init):
    hw = 2 * DIFF_HEAD_DIM
    assert tbl.shape[2] == tq and tk % tq == 0 and seq % tk == 0
    return pl.pallas_call(
        functools.partial(_diff_attn_kernel, tq=tq, tk=tk, seq=seq, lambda_init=lambda_init),
        out_shape=jax.ShapeDtypeStruct((batch, seq, n_heads * hw), BF16),
        grid=(batch, n_heads, seq // tq),
        in_specs=[
            pl.BlockSpec(memory_space=pltpu.SMEM),
            pl.BlockSpec((1, tq, hw), lambda b, h, i: (b, i, h)),
            pl.BlockSpec((1, seq, hw), lambda b, h, i: (b, 0, n_heads + h)),
            pl.BlockSpec((1, seq, hw), lambda b, h, i: (b, 0, 2 * n_heads + h)),
            pl.BlockSpec((1, N_BIAS_TILES, tq, tq), lambda b, h, i: (h, 0, 0, 0)),
            pl.BlockSpec((1, hw), lambda b, h, i: (0, 0)),
        ],
        out_specs=pl.BlockSpec((1, tq, hw), lambda b, h, i: (b, i, h)),
        compiler_params=_params("arbitrary", "arbitrary", "arbitrary"),
        name="diff_attention",
    )(lam, u, u, u, tbl, g_subln.reshape(1, hw))


def _conv_kernel(a_ref, ap_ref, an_ref, g_ref, gp_ref, gn_ref, w_ref, b_ref, lg_ref, lb_ref,
                 o_ref, z_ref, zs_ref, y_ref, *, ts):
    i = pl.program_id(1)
    last = pl.num_programs(1) - 1

    def glu(a, g):
        return a[0].astype(F32) * jax.nn.sigmoid(g[0].astype(F32))

    z_ref[:HALO_ROWS, :] = glu(ap_ref, gp_ref) * (i > 0).astype(F32)
    z_ref[HALO_ROWS:HALO_ROWS + ts, :] = glu(a_ref, g_ref)
    z_ref[HALO_ROWS + ts:, :] = glu(an_ref, gn_ref) * (i < last).astype(F32)
    first = HALO_ROWS - CONV_KERNEL // 2
    span = zs_ref.shape[1]
    for res in range(1, 8):
        zs_ref[res - 1] = z_ref[res:res + span, :]
    n_cblocks = y_ref.shape[1] // CONV_CBLOCK
    for r0 in range(0, ts, CONV_RBLOCK):
        def channel_block(cb, carry, r0=r0):
            cols = pl.ds(pl.multiple_of(cb * CONV_CBLOCK, CONV_CBLOCK), CONV_CBLOCK)
            acc = jnp.zeros((CONV_RBLOCK, CONV_CBLOCK), F32)
            for t in range(CONV_KERNEL):
                res, base = (first + t) % 8, (first + t) // 8 * 8 + r0
                rows = slice(base, base + CONV_RBLOCK)
                shifted = z_ref[rows, cols] if res == 0 else zs_ref[res - 1, rows, cols]
                acc = acc + w_ref[t:t + 1, cols] * shifted
            y_ref[r0:r0 + CONV_RBLOCK, cols] = acc + b_ref[:, cols]
            return carry
        lax.fori_loop(0, n_cblocks, channel_block, 0)
    y = y_ref[...]
    yc = y - jnp.mean(y, axis=-1, keepdims=True)
    yn = yc * lax.rsqrt(jnp.mean(yc * yc, axis=-1, keepdims=True) + EPS)
    yn = yn * lg_ref[...] + lb_ref[...]
    o_ref[0] = (yn * jax.nn.sigmoid(yn)).astype(o_ref.dtype)


def _conv_module(u, w_dw, b_dw, g_ln, b_ln, *, batch, seq, col_block, ts):
    cw = w_dw.shape[1]
    hb = ts // HALO_ROWS
    n_hblocks = seq // HALO_ROWS

    def main(c):
        return pl.BlockSpec((1, ts, cw), lambda b, i: (b, i, c))

    def prev(c):
        return pl.BlockSpec((1, HALO_ROWS, cw), lambda b, i: (b, jnp.maximum(i * hb - 1, 0), c))

    def nxt(c):
        return pl.BlockSpec((1, HALO_ROWS, cw),
                            lambda b, i: (b, jnp.minimum((i + 1) * hb, n_hblocks - 1), c))

    row = pl.BlockSpec((1, cw), lambda b, i: (0, 0))
    return pl.pallas_call(
        functools.partial(_conv_kernel, ts=ts),
        out_shape=jax.ShapeDtypeStruct((batch, seq, cw), BF16),
        grid=(batch, seq // ts),
        in_specs=[main(col_block), prev(col_block), nxt(col_block),
                  main(col_block + 1), prev(col_block + 1), nxt(col_block + 1),
                  pl.BlockSpec((CONV_KERNEL, cw), lambda b, i: (0, 0)), row, row, row],
        out_specs=pl.BlockSpec((1, ts, cw), lambda b, i: (b, i, 0)),
        scratch_shapes=[pltpu.VMEM((ts + 2 * HALO_ROWS, cw), F32),
                        pltpu.VMEM((7, ts + 2 * HALO_ROWS - 8, cw), F32),
                        pltpu.VMEM((ts, cw), F32)],
        compiler_params=_params("arbitrary", "arbitrary"),
        name="conv_module",
    )(u, u, u, u, u, u, w_dw, b_dw.reshape(1, cw), g_ln.reshape(1, cw), b_ln.reshape(1, cw))


def _cross_attn_kernel(q_ref, k_ref, v_ref, o_ref, *, head_dim):
    scale = head_dim ** -0.5
    for h in range(N_CROSS_HEADS):
        cols = slice(h * head_dim, (h + 1) * head_dim)
        s = lax.dot_general(q_ref[:, cols], k_ref[:, cols], (((1,), (1,)), ((), ())),
                            preferred_element_type=F32) * scale
        e = jnp.exp(s - jnp.max(s, axis=-1, keepdims=True))
        p = e * (1.0 / jnp.sum(e, axis=-1, keepdims=True))
        o_ref[:, cols] = jnp.dot(p.astype(BF16), v_ref[:, cols],
                                 preferred_element_type=F32).astype(o_ref.dtype)


def _cross_attention(q, k, v, *, batch, seq, n_mem, tm):
    width = q.shape[1]
    qb = seq // tm
    return pl.pallas_call(
        functools.partial(_cross_attn_kernel, head_dim=width // N_CROSS_HEADS),
        out_shape=jax.ShapeDtypeStruct(q.shape, BF16),
        grid=(batch, qb),
        in_specs=[pl.BlockSpec((tm, width), lambda b, i: (b * qb + i, 0)),
                  pl.BlockSpec((n_mem, width), lambda b, i: (b, 0)),
                  pl.BlockSpec((n_mem, width), lambda b, i: (b, 0))],
        out_specs=pl.BlockSpec((tm, width), lambda b, i: (b * qb + i, 0)),
        compiler_params=_params("arbitrary", "arbitrary"),
        name="cross_attention",
    )(q, k, v)


def _pack_bf16_pairs(lo_half, hi_half):
    a = pltpu.bitcast(lo_half, U32)
    b = pltpu.bitcast(hi_half, U32)
    return (a & jnp.uint32(0xFFFF0000)) | (b >> 16)


def _unpack_bf16_pairs(words):
    a = pltpu.bitcast(words & jnp.uint32(0xFFFF0000), F32)
    b = pltpu.bitcast(words << 16, F32)
    return a.astype(BF16), b.astype(BF16)


def _router_kernel(x_ref, g_ref, w_ref, b_ref, hp_ref, idx_ref, gate_ref, rank_ref, cnt_ref,
                   seen_ref):
    @pl.when(pl.program_id(0) == 0)
    def _():
        seen_ref[...] = jnp.zeros(seen_ref.shape, F32)

    x = x_ref[...]
    tm, d = x.shape
    y = x * lax.rsqrt(jnp.mean(x * x, axis=-1, keepdims=True) + EPS)
    hb = (y * g_ref[...]).astype(BF16)
    hp_ref[...] = _pack_bf16_pairs(hb[:, :d // 2].astype(F32), hb[:, d // 2:].astype(F32))
    logits = jnp.dot(hb, w_ref[...].astype(BF16), preferred_element_type=F32) + b_ref[...]
    n_exp = logits.shape[1]
    lane = lax.broadcasted_iota(jnp.int32, logits.shape, 1)
    kl = lax.broadcasted_iota(jnp.int32, idx_ref.shape, 1)
    vals = logits
    top_v = jnp.zeros(gate_ref.shape, F32)
    top_i = jnp.zeros(idx_ref.shape, jnp.int32)
    picks = []
    for kk in range(TOP_K):
        mx = jnp.max(vals, axis=-1, keepdims=True)
        sel = jnp.min(jnp.where(vals == mx, lane, n_exp), axis=-1, keepdims=True)
        top_v = jnp.where(kl == kk, mx, top_v)
        top_i = jnp.where(kl == kk, sel, top_i)
        picks.append(lane == sel)
        vals = jnp.where(picks[-1], -jnp.inf, vals)
    e = jnp.exp(top_v - jnp.max(top_v, axis=-1, keepdims=True))
    gate_ref[...] = e / jnp.sum(e, axis=-1, keepdims=True)
    idx_ref[...] = top_i
    member = jnp.zeros(logits.shape, F32)
    for pick in picks:
        member = member + pick.astype(F32)
    earlier = (lax.broadcasted_iota(jnp.int32, (tm, tm), 0)
               > lax.broadcasted_iota(jnp.int32, (tm, tm), 1)).astype(BF16)
    before = seen_ref[...] + jnp.dot(earlier, member.astype(BF16), preferred_element_type=F32)
    rank = jnp.zeros(rank_ref.shape, F32)
    for kk, pick in enumerate(picks):
        rank = jnp.where(kl == kk, jnp.sum(jnp.where(pick, before, 0.0), axis=-1, keepdims=True),
                         rank)
    rank_ref[...] = rank.astype(jnp.int32)
    seen_ref[...] = seen_ref[...] + jnp.sum(member, axis=0, keepdims=True)
    cnt_ref[...] = seen_ref[...].astype(jnp.int32)


def _router(x, g, w_router, b_router, layer, *, tm):
    n, d = x.shape
    n_exp = w_router.shape[2]
    per_tok = pl.BlockSpec((tm, TOP_K), lambda i: (i, 0))
    return pl.pallas_call(
        _router_kernel,
        out_shape=(jax.ShapeDtypeStruct((n, d // 2), U32),
                   jax.ShapeDtypeStruct((n, TOP_K), jnp.int32),
                   jax.ShapeDtypeStruct((n, TOP_K), F32),
                   jax.ShapeDtypeStruct((n, TOP_K), jnp.int32),
                   jax.ShapeDtypeStruct((1, n_exp), jnp.int32)),
        grid=(n // tm,),
        in_specs=[pl.BlockSpec((tm, d), lambda i: (i, 0)),
                  pl.BlockSpec((1, d), lambda i: (0, 0)),
                  pl.BlockSpec((None, d, n_exp), lambda i: (layer, 0, 0)),
                  pl.BlockSpec((1, n_exp), lambda i: (0, 0))],
        out_specs=(pl.BlockSpec((tm, d // 2), lambda i: (i, 0)), per_tok, per_tok, per_tok,
                   pl.BlockSpec((1, n_exp), lambda i: (0, 0))),
        scratch_shapes=[pltpu.VMEM((1, n_exp), F32)],
        compiler_params=_params("arbitrary"),
        name="router",
    )(x, g.reshape(1, d), w_router, b_router.reshape(1, n_exp))


def _dispatch_kernel(zt_ref, dest_ref, src_ref, dst_ref, zero_ref, zsem, sem, *, tt, tm):
    @pl.when(pl.program_id(0) == 0)
    def _():
        zero_ref[...] = jnp.zeros(zero_ref.shape, zero_ref.dtype)

        def zero_copy(i):
            return pltpu.make_async_copy(zero_ref, dst_ref.at[pl.ds(zt_ref[i] * tm, tm)], zsem.at[0])

        def start(i, carry):
            @pl.when(zt_ref[i] >= 0)
            def _():
                zero_copy(i).start()
            return carry

        def wait(i, carry):
            @pl.when(zt_ref[i] >= 0)
            def _():
                zero_copy(i).wait()
            return carry

        lax.fori_loop(0, zt_ref.shape[0], start, 0)
        lax.fori_loop(0, zt_ref.shape[0], wait, 0)

    def row_copy(r, kk):
        return pltpu.make_async_copy(src_ref.at[pl.ds(r, 1)],
                                     dst_ref.at[pl.ds(dest_ref[0, 0, r * TOP_K + kk], 1)],
                                     sem.at[0])

    for r in range(tt):
        for kk in range(TOP_K):
            row_copy(r, kk).start(priority=kk % 2)
    for kk in range(TOP_K):
        pltpu.make_async_copy(src_ref, dst_ref.at[pl.ds(0, tt)], sem.at[0]).wait()


def _dispatch(src, dest, zero_tiles, *, p, tt, tm):
    n, half = src.shape
    nt = n // tt
    return pl.pallas_call(
        functools.partial(_dispatch_kernel, tt=tt, tm=tm),
        out_shape=jax.ShapeDtypeStruct((p, half), src.dtype),
        grid_spec=pltpu.PrefetchScalarGridSpec(
            num_scalar_prefetch=1,
            grid=(nt,),
            in_specs=[pl.BlockSpec((1, 1, tt * TOP_K), lambda t, zt: (t, 0, 0),
                                   memory_space=pltpu.SMEM),
                      pl.BlockSpec((tt, half), lambda t, zt: (t, 0))],
            out_specs=pl.BlockSpec(memory_space=pl.ANY),
            scratch_shapes=[pltpu.VMEM((tm, half), src.dtype),
                            pltpu.SemaphoreType.DMA((1,)), pltpu.SemaphoreType.DMA((1,))],
        ),
        compiler_params=_params("arbitrary"),
        name="moe_dispatch",
    )(zero_tiles, dest.reshape(nt, 1, tt * TOP_K), src)


def _is_first_tile_of_expert(te_ref, t):
    return jnp.logical_or(t == 0, te_ref[t] != te_ref[jnp.maximum(t - 1, 0)])


def _gmm1_kernel(te_ref, nu_ref, nx_ref, x_ref, w_ref, ba_ref, bb_ref, o_ref,
                 stage_a, stage_b, wa_bf, wb_bf, sem, *, layer, tf, nj):
    j = pl.program_id(0)
    t = pl.program_id(1)

    def fetch(e, jj):
        cols_a = pl.ds(pl.multiple_of(jj * tf, tf), tf)
        cols_b = pl.ds(pl.multiple_of((nj + jj) * tf, tf), tf)
        return (pltpu.make_async_copy(w_ref.at[layer, e, :, cols_a], stage_a, sem.at[0]),
                pltpu.make_async_copy(w_ref.at[layer, e, :, cols_b], stage_b, sem.at[1]))

    @pl.when(t < nu_ref[0])
    def _():
        @pl.when(_is_first_tile_of_expert(te_ref, t))
        def _():
            @pl.when(jnp.logical_and(j == 0, t == 0))
            def _():
                for copy in fetch(te_ref[0], 0):
                    copy.start()

            for copy in fetch(te_ref[t], j):
                copy.wait()
            wa_bf[...] = stage_a[...].astype(BF16)
            wb_bf[...] = stage_b[...].astype(BF16)
            more_here = nx_ref[t] >= 0
            next_e = jnp.where(more_here, nx_ref[t], te_ref[0])
            next_j = jnp.where(more_here, j, j + 1)

            @pl.when(next_j < nj)
            def _():
                for copy in fetch(next_e, next_j):
                    copy.start()

        x_lo, x_hi = _unpack_bf16_pairs(x_ref[...])
        half = x_lo.shape[1]

        def proj(w_bf, b_ref):
            return (jnp.dot(x_lo, w_bf[:half, :], preferred_element_type=F32)
                    + jnp.dot(x_hi, w_bf[half:, :], preferred_element_type=F32) + b_ref[...])

        a = jnp.minimum(proj(wa_bf, ba_ref), SWIGLU_LIMIT)
        b = jnp.clip(proj(wb_bf, bb_ref), -SWIGLU_LIMIT, SWIGLU_LIMIT)
        o_ref[...] = (a * jax.nn.sigmoid(SWIGLU_ALPHA * a) * (b + 1.0)).astype(o_ref.dtype)


def _gmm1(xg, w1, b1, layer, tile_e, n_used, next_e, *, tm, tf):
    p, half = xg.shape
    d = 2 * half
    n_exp, ff = w1.shape[1], w1.shape[3] // 2
    nj = ff // tf
    assert ff % tf == 0 and p % tm == 0

    def row(t, nu):
        return jnp.minimum(t, nu[0] - 1)

    return pl.pallas_call(
        functools.partial(_gmm1_kernel, layer=layer, tf=tf, nj=nj),
        out_shape=jax.ShapeDtypeStruct((p, ff), BF16),
        grid_spec=pltpu.PrefetchScalarGridSpec(
            num_scalar_prefetch=3,
            grid=(nj, p // tm),
            in_specs=[
                pl.BlockSpec((tm, half), lambda j, t, te, nu, nx: (row(t, nu), 0)),
                pl.BlockSpec(memory_space=pl.ANY),
                pl.BlockSpec((None, None, 1, tf), lambda j, t, te, nu, nx: (layer, te[t], 0, j)),
                pl.BlockSpec((None, None, 1, tf),
                             lambda j, t, te, nu, nx: (layer, te[t], 0, nj + j)),
            ],
            out_specs=pl.BlockSpec((tm, tf), lambda j, t, te, nu, nx: (row(t, nu), j)),
            scratch_shapes=[pltpu.VMEM((d, tf), F32), pltpu.VMEM((d, tf), F32),
                            pltpu.VMEM((d, tf), BF16), pltpu.VMEM((d, tf), BF16),
                            pltpu.SemaphoreType.DMA((2,))],
        ),
        compiler_params=_params("arbitrary", "arbitrary"),
        name="moe_gmm1",
    )(tile_e, n_used, next_e, xg, w1, b1.reshape(b1.shape[0], n_exp, 1, 2 * ff),
      b1.reshape(b1.shape[0], n_exp, 1, 2 * ff))


def _gmm2_kernel(te_ref, nu_ref, x_ref, w_ref, b_ref, o_ref, w_bf):
    t = pl.program_id(1)

    @pl.when(t < nu_ref[0])
    def _():
        @pl.when(_is_first_tile_of_expert(te_ref, t))
        def _():
            w_bf[...] = w_ref[...].astype(BF16)

        y = jnp.dot(x_ref[...], w_bf[...], preferred_element_type=F32) + b_ref[...]
        yb = y.astype(BF16).astype(F32)
        half = yb.shape[1] // 2
        o_ref[...] = _pack_bf16_pairs(yb[:, :half], yb[:, half:])


def _gmm2(act, w2, b2, layer, tile_e, n_used, *, tm, tn):
    p, ff = act.shape
    n_exp, d = w2.shape[1], w2.shape[3]
    assert d % tn == 0

    def row(t, nu):
        return jnp.minimum(t, nu[0] - 1)

    return pl.pallas_call(
        _gmm2_kernel,
        out_shape=jax.ShapeDtypeStruct((p, d // 2), U32),
        grid_spec=pltpu.PrefetchScalarGridSpec(
            num_scalar_prefetch=2,
            grid=(d // tn, p // tm),
            in_specs=[
                pl.BlockSpec((tm, ff), lambda j, t, te, nu: (row(t, nu), 0)),
                pl.BlockSpec((None, None, ff, tn), lambda j, t, te, nu: (layer, te[t], 0, j)),
                pl.BlockSpec((None, None, 1, tn), lambda j, t, te, nu: (layer, te[t], 0, j)),
            ],
            out_specs=pl.BlockSpec((tm, tn // 2), lambda j, t, te, nu: (row(t, nu), j)),
            scratch_shapes=[pltpu.VMEM((ff, tn), BF16)],
        ),
        compiler_params=_params("arbitrary", "arbitrary"),
        name="moe_gmm2",
    )(tile_e, n_used, act, w2, b2.reshape(b2.shape[0], n_exp, 1, d))


def _unpack_expert_rows(words, pack_cols):
    lo = pltpu.bitcast(words & jnp.uint32(0xFFFF0000), F32)
    hi = pltpu.bitcast(words << 16, F32)
    half = pack_cols // 2
    pieces = []
    for c in range(0, words.shape[1], half):
        pieces += [lo[:, c:c + half], hi[:, c:c + half]]
    return jnp.concatenate(pieces, axis=1)


def _combine_kernel(d0_ref, d1_ref, d2_ref, x_ref, gate_ref, g_ref, y_ref, o_ref, buf, sem, *,
                    tt, pack_cols, final_norm):
    t = pl.program_id(0)
    last = pl.num_programs(0) - 1
    slot = lax.rem(t, COMBINE_SLOTS)
    ahead = lax.rem(t + 2, COMBINE_SLOTS)

    def row_copy(d_ref, r, kk, s):
        return pltpu.make_async_copy(y_ref.at[pl.ds(d_ref[0, 0, r * TOP_K + kk], 1)],
                                     buf.at[s, kk, pl.ds(r, 1)], sem.at[s])

    def issue(d_ref, s, rows):
        for r in rows:
            for kk in range(TOP_K):
                row_copy(d_ref, r, kk, s).start(priority=kk % 2)

    def drain(s):
        for kk in range(TOP_K):
            pltpu.make_async_copy(y_ref.at[pl.ds(0, tt)], buf.at[s, kk], sem.at[s]).wait()

    @pl.when(t == 0)
    def _():
        issue(d0_ref, 0, range(tt))
        issue(d1_ref, 1, range(tt))

    drain(slot)
    gate = gate_ref[...]
    for r0 in range(0, tt, COMBINE_GROUP):
        rows = slice(r0, r0 + COMBINE_GROUP)
        acc = x_ref[rows, :]
        parts = [_unpack_expert_rows(buf[slot, kk, rows, :], pack_cols) for kk in range(TOP_K)]
        issue(d2_ref, ahead, range(r0, r0 + COMBINE_GROUP))
        for kk in range(TOP_K):
            acc = acc + gate[rows, kk:kk + 1] * parts[kk]
        if final_norm:
            acc = acc * lax.rsqrt(jnp.mean(acc * acc, axis=-1, keepdims=True) + EPS) * g_ref[...]
        o_ref[rows, :] = acc

    @pl.when(t == last)
    def _():
        drain(lax.rem(t + 1, COMBINE_SLOTS))
        drain(ahead)


def _combine(x, y, dest, gate, g_final, *, tt, pack_cols, final_norm):
    n, d = x.shape
    nt = n // tt
    assert nt >= COMBINE_SLOTS and tt % COMBINE_GROUP == 0
    dest3 = dest.reshape(nt, 1, tt * TOP_K)

    def tile(k):
        return pl.BlockSpec((1, 1, tt * TOP_K), lambda t: (jnp.minimum(t + k, nt - 1), 0, 0),
                            memory_space=pltpu.SMEM)

    return pl.pallas_call(
        functools.partial(_combine_kernel, tt=tt, pack_cols=pack_cols, final_norm=final_norm),
        out_shape=jax.ShapeDtypeStruct((n, d), F32),
        grid=(nt,),
        in_specs=[
            tile(0), tile(1), tile(2),
            pl.BlockSpec((tt, d), lambda t: (t, 0)),
            pl.BlockSpec((tt, TOP_K), lambda t: (t, 0)),
            pl.BlockSpec((1, d), lambda t: (0, 0)),
            pl.BlockSpec(memory_space=pl.ANY),
        ],
        out_specs=pl.BlockSpec((tt, d), lambda t: (t, 0)),
        scratch_shapes=[pltpu.VMEM((COMBINE_SLOTS, TOP_K, tt, d // 2), U32),
                        pltpu.SemaphoreType.DMA((COMBINE_SLOTS,))],
        compiler_params=_params("arbitrary"),
        name="moe_combine",
    )(dest3, dest3, dest3, x, gate, g_final.reshape(1, d), y)


def _routing_tables(top_idx, rank, counts, tm):
    n = top_idx.shape[0]
    n_exp = counts.shape[0]
    n_tiles = (n * TOP_K) // tm + n_exp
    padded = (counts + tm - 1) // tm * tm
    pad_end = jnp.cumsum(padded)
    pad_start = pad_end - padded
    experts = jnp.arange(n_exp, dtype=jnp.int32)
    dest = rank + jnp.sum(jnp.where(top_idx[..., None] == experts, pad_start, 0), axis=-1)
    tile_start = jnp.arange(n_tiles, dtype=jnp.int32) * tm
    tile_e = jnp.minimum(jnp.sum(tile_start[:, None] >= pad_end[None, :], axis=-1),
                         n_exp - 1).astype(jnp.int32)
    n_used = pad_end[-1:] // tm
    later = jnp.logical_and(experts[None, :] > tile_e[:, None], counts[None, :] > 0)
    next_e = jnp.min(jnp.where(later, experts[None, :], n_exp), axis=-1)
    next_e = jnp.where(next_e < n_exp, next_e, -1)
    ragged = jnp.where(counts % tm != 0, pad_end // tm - 1, -1)
    tail = n_used + experts
    zero_tiles = jnp.concatenate([ragged, jnp.where(tail < n_tiles, tail, -1)])
    return (dest.astype(jnp.int32), tile_e, n_used.astype(jnp.int32), next_e.astype(jnp.int32),
            zero_tiles.astype(jnp.int32), n_tiles * tm)


def _pick(n, pref):
    t = min(n, pref)
    while n % t or t % 8:
        t -= 1
    return t


def kernel(x, mem, g_mix, w_in, lambda_q1, lambda_k1, lambda_q2, lambda_k2, g_subln, w_dw, b_dw,
           g_conv_ln, b_conv_ln, w_out, rel_bias, g_cross, g_mem, w_cq, w_ck, w_cv, w_co, g_ffn,
           w_router, b_router, w1, b1, w2, b2, g_final):
    batch, seq, d = x.shape
    n = batch * seq
    n_mem = mem.shape[1]
    depth = g_mix.shape[0]
    n_heads = rel_bias.shape[1]
    attn_w = n_heads * 2 * DIFF_HEAD_DIM
    conv_w = w_dw.shape[2]
    in_w = w_in.shape[2]
    n_exp = w_router.shape[2]
    assert in_w == 3 * attn_w + 2 * conv_w and (3 * attn_w) % conv_w == 0

    tq = _pick(seq, 512)
    xf = x.reshape(n, d)
    memf = mem.reshape(batch * n_mem, d)
    for l in range(depth):
        lambda_init = 0.8 - 0.6 * math.exp(-0.3 * l)
        tbl, lam = _attn_prep(rel_bias, lambda_q1[l], lambda_k1[l], lambda_q2[l], lambda_k2[l],
                              tb=tq, lambda_init=lambda_init)
        h = _rmsnorm(xf, g_mix[l], rows=_pick(n, 256))
        u = _dense([h], w_in, l, tm=_pick(n, 512), tn=_pick(in_w, 1024), out_dtype=BF16,
                   name="in_proj").reshape(batch, seq, in_w)
        attn = _diff_attention(u, tbl, lam, g_subln[l], batch=batch, seq=seq, n_heads=n_heads,
                               tq=tq, tk=_pick(seq, 512), lambda_init=lambda_init)
        conv = _conv_module(u, w_dw[l], b_dw[l], g_conv_ln[l], b_conv_ln[l], batch=batch, seq=seq,
                            col_block=3 * attn_w // conv_w, ts=_pick(seq, 256))
        xf = _dense([attn.reshape(n, attn_w), conv.reshape(n, conv_w)], w_out, l,
                    tm=_pick(n, 1024), tn=_pick(d, 512), out_dtype=F32, res=xf, name="out_proj")
        m = _rmsnorm(memf, g_mem[l], rows=_pick(batch * n_mem, 256))
        cross_w = w_cq.shape[2]
        qc = _dense([xf], w_cq, l, tm=_pick(n, 256), tn=_pick(cross_w, 1024), out_dtype=BF16,
                    norm_g=g_cross[l], name="cross_q")
        kc = _dense([m], w_ck, l, tm=_pick(batch * n_mem, 512), tn=_pick(cross_w, 512),
                    out_dtype=BF16, name="cross_k")
        vc = _dense([m], w_cv, l, tm=_pick(batch * n_mem, 512), tn=_pick(cross_w, 512),
                    out_dtype=BF16, name="cross_v")
        oc = _cross_attention(qc, kc, vc, batch=batch, seq=seq, n_mem=n_mem, tm=_pick(seq, 512))
        xf = _dense([oc], w_co, l, tm=_pick(n, 512), tn=_pick(d, 2048), out_dtype=F32, res=xf,
                    name="cross_o")
        tm = 512
        hp, top_idx, gate, rank, counts = _router(xf, g_ffn[l], w_router, b_router[l], l,
                                                  tm=_pick(n, 256))
        dest, tile_e, n_used, next_e, zero_tiles, p_rows = _routing_tables(
            top_idx, rank, counts[0], tm)
        xg = _dispatch(hp, dest, zero_tiles, p=p_rows, tt=_pick(n, 128), tm=tm)
        ff = w2.shape[2]
        act = _gmm1(xg, w1, b1, l, tile_e, n_used, next_e, tm=tm, tf=_pick(ff, 512))
        tn = _pick(d, 2048)
        y = _gmm2(act, w2, b2, l, tile_e, n_used, tm=tm, tn=tn)
        xf = _combine(xf, y, dest, gate, g_final, tt=_pick(n, 128), pack_cols=tn,
                      final_norm=l == depth - 1)
    return xf.reshape(batch, seq, d)
```

```python
import functools
import math

import jax
import jax.numpy as jnp
from jax import lax
from jax.experimental import pallas as pl
from jax.experimental.pallas import tpu as pltpu

F32 = jnp.float32
BF16 = jnp.bfloat16
U32 = jnp.uint32

DIFF_HEAD_DIM = 128
CONV_KERNEL = 31
N_BUCKETS = 32
MAX_DISTANCE = 128
N_CROSS_HEADS = 4
TOP_K = 4
SWIGLU_LIMIT = 7.0
SWIGLU_ALPHA = 1.702
EPS = 1e-6
LOG2_E = math.log2(math.e)
N_BIAS_TILES = 5

V7X_VMEM_BYTES = 64 * 1024 * 1024
VMEM_LIMIT_BYTES = V7X_VMEM_BYTES - 6 * 1024 * 1024
LANES = 128
COMBINE_SLOTS = 3
COMBINE_GROUP = 8
CONV_RBLOCK = 64
CONV_CBLOCK = 256
HALO_ROWS = 16

_T5_LOG_THRESHOLDS = tuple(
    math.ceil(8 * (MAX_DISTANCE / 8) ** (k / 8) - 1e-9) for k in range(1, 8))


def _params(*semantics):
    return pltpu.CompilerParams(dimension_semantics=semantics,
                                vmem_limit_bytes=VMEM_LIMIT_BYTES)


def _rmsnorm_kernel(x_ref, g_ref, o_ref):
    x = x_ref[...]
    y = x * lax.rsqrt(jnp.mean(x * x, axis=-1, keepdims=True) + EPS)
    o_ref[...] = (y * g_ref[...]).astype(o_ref.dtype)


def _rmsnorm(x, g, *, rows):
    n, d = x.shape
    return pl.pallas_call(
        _rmsnorm_kernel,
        out_shape=jax.ShapeDtypeStruct((n, d), BF16),
        grid=(n // rows,),
        in_specs=[pl.BlockSpec((rows, d), lambda i: (i, 0)),
                  pl.BlockSpec((1, d), lambda i: (0, 0))],
        out_specs=pl.BlockSpec((rows, d), lambda i: (i, 0)),
        compiler_params=_params("arbitrary"),
        name="rmsnorm",
    )(x, g.reshape(1, d))


def _dense_kernel(*refs, k_splits, has_norm, has_res):
    n_lhs = len(k_splits)
    x_refs = refs[:n_lhs]
    rest = list(refs[n_lhs:-2])
    g_ref = rest.pop(0) if has_norm else None
    w_ref = rest.pop(0)
    res_ref = rest.pop(0) if has_res else None
    o_ref, wbf_ref = refs[-2], refs[-1]

    @pl.when(pl.program_id(1) == 0)
    def _():
        wbf_ref[...] = w_ref[...].astype(BF16)

    acc = None
    k0 = 0
    for x_ref, kw in zip(x_refs, k_splits):
        x = x_ref[...]
        if has_norm:
            x = x * lax.rsqrt(jnp.mean(x * x, axis=-1, keepdims=True) + EPS)
            x = (x * g_ref[...]).astype(BF16)
        part = jnp.dot(x, wbf_ref[k0:k0 + kw, :], preferred_element_type=F32)
        acc = part if acc is None else acc + part
        k0 += kw
    if has_res:
        acc = res_ref[...] + acc
    o_ref[...] = acc.astype(o_ref.dtype)


def _dense(xs, w, layer, *, tm, tn, out_dtype, norm_g=None, res=None, name):
    m = xs[0].shape[0]
    k_splits = tuple(x.shape[1] for x in xs)
    k, n = w.shape[1], w.shape[2]
    assert sum(k_splits) == k and m % tm == 0 and n % tn == 0
    assert norm_g is None or len(xs) == 1
    in_specs = [pl.BlockSpec((tm, kw), lambda j, i: (i, 0)) for kw in k_splits]
    args = list(xs)
    if norm_g is not None:
        in_specs.append(pl.BlockSpec((1, k), lambda j, i: (0, 0)))
        args.append(norm_g.reshape(1, k))
    in_specs.append(pl.BlockSpec((None, k, tn), lambda j, i: (layer, 0, j)))
    args.append(w)
    if res is not None:
        in_specs.append(pl.BlockSpec((tm, tn), lambda j, i: (i, j)))
        args.append(res)
    return pl.pallas_call(
        functools.partial(_dense_kernel, k_splits=k_splits, has_norm=norm_g is not None,
                          has_res=res is not None),
        out_shape=jax.ShapeDtypeStruct((m, n), out_dtype),
        grid=(n // tn, m // tm),
        in_specs=in_specs,
        out_specs=pl.BlockSpec((tm, tn), lambda j, i: (i, j)),
        scratch_shapes=[pltpu.VMEM((k, tn), BF16)],
        compiler_params=_params("arbitrary", "arbitrary"),
        name=name,
    )(*args)


def _t5_bucket(rel):
    half = N_BUCKETS // 2
    max_exact = half // 2
    n = jnp.abs(rel)
    large = jnp.full(rel.shape, max_exact, jnp.int32)
    for thr in _T5_LOG_THRESHOLDS:
        large = large + (n >= thr).astype(jnp.int32)
    return jnp.where(rel > 0, half, 0) + jnp.where(n < max_exact, n, large)


def _t5_bucket_static(rel):
    half = N_BUCKETS // 2
    n = abs(rel)
    large = half // 2 + sum(n >= thr for thr in _T5_LOG_THRESHOLDS)
    return (half if rel > 0 else 0) + (n if n < half // 2 else large)


def _attn_prep_kernel(rb_ref, lq1_ref, lk1_ref, lq2_ref, lk2_ref, tbl_ref, lam_ref, *,
                      tb, lambda_init):
    h = pl.program_id(0)
    a = lax.broadcasted_iota(jnp.int32, (tb, tb), 0)
    b = lax.broadcasted_iota(jnp.int32, (tb, tb), 1)
    for u in range(N_BIAS_TILES):
        origin = (u - N_BIAS_TILES // 2) * tb
        reachable = sorted({_t5_bucket_static(rel)
                            for rel in range(origin - tb + 1, origin + tb)})
        bucket = _t5_bucket(origin + b - a)
        val = jnp.full((tb, tb), rb_ref[reachable[0], h], F32)
        for bk in reachable[1:]:
            val = jnp.where(bucket == bk, rb_ref[bk, h], val)
        tbl_ref[0, u] = val * LOG2_E
    s1 = jnp.sum(lq1_ref[...] * lk1_ref[...], axis=-1, keepdims=True)
    s2 = jnp.sum(lq2_ref[...] * lk2_ref[...], axis=-1, keepdims=True)
    lam_ref[...] = jnp.exp(s1) - jnp.exp(s2) + lambda_init


def _attn_prep(rel_bias, lq1, lk1, lq2, lk2, *, tb, lambda_init):
    n_heads = rel_bias.shape[1]
    assert tb > MAX_DISTANCE
    vec = pl.BlockSpec((1, DIFF_HEAD_DIM), lambda h: (0, 0))
    return pl.pallas_call(
        functools.partial(_attn_prep_kernel, tb=tb, lambda_init=lambda_init),
        out_shape=(jax.ShapeDtypeStruct((n_heads, N_BIAS_TILES, tb, tb), F32),
                   jax.ShapeDtypeStruct((1, 1), F32)),
        grid=(n_heads,),
        in_specs=[pl.BlockSpec(memory_space=pltpu.SMEM), vec, vec, vec, vec],
        out_specs=(pl.BlockSpec((1, N_BIAS_TILES, tb, tb), lambda h: (h, 0, 0, 0)),
                   pl.BlockSpec((1, 1), lambda h: (0, 0))),
        compiler_params=_params("arbitrary"),
        name="attn_prep",
    )(rel_bias, lq1.reshape(1, -1), lk1.reshape(1, -1), lq2.reshape(1, -1), lk2.reshape(1, -1))


def _diff_attn_kernel(lam_ref, q_ref, k_ref, v_ref, tbl_ref, g_ref, o_ref, *,
                      tq, tk, seq, lambda_init):
    i = pl.program_id(2)
    dh = DIFF_HEAD_DIM
    scale = dh ** -0.5 * LOG2_E
    q = q_ref[0]
    first_half = lax.broadcasted_iota(jnp.int32, q.shape, 1) < dh
    zero = jnp.zeros(q.shape, q.dtype)
    qq = jnp.concatenate([jnp.where(first_half, q, zero), jnp.where(first_half, zero, q)], axis=0)
    tiles_per_chunk = tk // tq
    parts = []
    for c in range(seq // tk):
        keys = slice(c * tk, (c + 1) * tk)
        s = lax.dot_general(qq, k_ref[0, keys, :], (((1,), (1,)), ((), ())),
                            preferred_element_type=F32)
        bias = jnp.concatenate(
            [tbl_ref[0, jnp.clip(c * tiles_per_chunk + j - i + N_BIAS_TILES // 2,
                                 0, N_BIAS_TILES - 1)] for j in range(tiles_per_chunk)], axis=1)
        t = s * scale + jnp.concatenate([bias, bias], axis=0)
        m = jnp.max(t, axis=-1, keepdims=True)
        e = jnp.exp2(t - m)
        parts.append((m, jnp.sum(e, axis=-1, keepdims=True),
                      jnp.dot(e.astype(BF16), v_ref[0, keys, :], preferred_element_type=F32)))
    m_all = parts[0][0]
    for m, _, _ in parts[1:]:
        m_all = jnp.maximum(m_all, m)
    l_all = jnp.zeros_like(m_all)
    o = jnp.zeros(parts[0][2].shape, F32)
    for m, l, oc in parts:
        w = jnp.exp2(m - m_all)
        l_all = l_all + w * l
        o = o + w * oc
    o = o * (1.0 / l_all)
    o = o[:tq] - lam_ref[0, 0] * o[tq:]
    y = o * lax.rsqrt(jnp.mean(o * o, axis=-1, keepdims=True) + EPS)
    o_ref[0] = ((y * g_ref[...]) * (1.0 - lambda_init)).astype(o_ref.dtype)


def _diff_attention(u, tbl, lam, g_subln, *, batch, seq, n_heads, tq, tk, lambda_init):
    hw = 2 * DIFF_HEAD_DIM
    assert tbl.shape[2] == tq and tk % tq == 0 and seq % tk == 0
    return pl.pallas_call(
        functools.partial(_diff_attn_kernel, tq=tq, tk=tk, seq=seq, lambda_init=lambda_init),
        out_shape=jax.ShapeDtypeStruct((batch, seq, n_heads * hw), BF16),
        grid=(batch, n_heads, seq // tq),
        in_specs=[
            pl.BlockSpec(memory_space=pltpu.SMEM),
            pl.BlockSpec((1, tq, hw), lambda b, h, i: (b, i, h)),
            pl.BlockSpec((1, seq, hw), lambda b, h, i: (b, 0, n_heads + h)),
            pl.BlockSpec((1, seq, hw), lambda b, h, i: (b, 0, 2 * n_heads + h)),
            pl.BlockSpec((1, N_BIAS_TILES, tq, tq), lambda b, h, i: (h, 0, 0, 0)),
            pl.BlockSpec((1, hw), lambda b, h, i: (0, 0)),
        ],
        out_specs=pl.BlockSpec((1, tq, hw), lambda b, h, i: (b, i, h)),
        compiler_params=_params("arbitrary", "arbitrary", "arbitrary"),
        name="diff_attention",
    )(lam, u, u, u, tbl, g_subln.reshape(1, hw))


def _conv_kernel(a_ref, ap_ref, an_ref, g_ref, gp_ref, gn_ref, w_ref, b_ref, lg_ref, lb_ref,
                 o_ref, z_ref, zs_ref, y_ref, *, ts):
    i = pl.program_id(1)
    last = pl.num_programs(1) - 1

    def glu(a, g):
        return a[0].astype(F32) * jax.nn.sigmoid(g[0].astype(F32))

    z_ref[:HALO_ROWS, :] = glu(ap_ref, gp_ref) * (i > 0).astype(F32)
    z_ref[HALO_ROWS:HALO_ROWS + ts, :] = glu(a_ref, g_ref)
    z_ref[HALO_ROWS + ts:, :] = glu(an_ref, gn_ref) * (i < last).astype(F32)
    first = HALO_ROWS - CONV_KERNEL // 2
    span = zs_ref.shape[1]
    for res in range(1, 8):
        zs_ref[res - 1] = z_ref[res:res + span, :]
    n_cblocks = y_ref.shape[1] // CONV_CBLOCK
    for r0 in range(0, ts, CONV_RBLOCK):
        def channel_block(cb, carry, r0=r0):
            cols = pl.ds(pl.multiple_of(cb * CONV_CBLOCK, CONV_CBLOCK), CONV_CBLOCK)
            acc = jnp.zeros((CONV_RBLOCK, CONV_CBLOCK), F32)
            for t in range(CONV_KERNEL):
                res, base = (first + t) % 8, (first + t) // 8 * 8 + r0
                rows = slice(base, base + CONV_RBLOCK)
                shifted = z_ref[rows, cols] if res == 0 else zs_ref[res - 1, rows, cols]
                acc = acc + w_ref[t:t + 1, cols] * shifted
            y_ref[r0:r0 + CONV_RBLOCK, cols] = acc + b_ref[:, cols]
            return carry
        lax.fori_loop(0, n_cblocks, channel_block, 0)
    y = y_ref[...]
    yc = y - jnp.mean(y, axis=-1, keepdims=True)
    yn = yc * lax.rsqrt(jnp.mean(yc * yc, axis=-1, keepdims=True) + EPS)
    yn = yn * lg_ref[...] + lb_ref[...]
    o_ref[0] = (yn * jax.nn.sigmoid(yn)).astype(o_ref.dtype)


def _conv_module(u, w_dw, b_dw, g_ln, b_ln, *, batch, seq, col_block, ts):
    cw = w_dw.shape[1]
    hb = ts // HALO_ROWS
    n_hblocks = seq // HALO_ROWS

    def main(c):
        return pl.BlockSpec((1, ts, cw), lambda b, i: (b, i, c))

    def prev(c):
        return pl.BlockSpec((1, HALO_ROWS, cw), lambda b, i: (b, jnp.maximum(i * hb - 1, 0), c))

    def nxt(c):
        return pl.BlockSpec((1, HALO_ROWS, cw),
                            lambda b, i: (b, jnp.minimum((i + 1) * hb, n_hblocks - 1), c))

    row = pl.BlockSpec((1, cw), lambda b, i: (0, 0))
    return pl.pallas_call(
        functools.partial(_conv_kernel, ts=ts),
        out_shape=jax.ShapeDtypeStruct((batch, seq, cw), BF16),
        grid=(batch, seq // ts),
        in_specs=[main(col_block), prev(col_block), nxt(col_block),
                  main(col_block + 1), prev(col_block + 1), nxt(col_block + 1),
                  pl.BlockSpec((CONV_KERNEL, cw), lambda b, i: (0, 0)), row, row, row],
        out_specs=pl.BlockSpec((1, ts, cw), lambda b, i: (b, i, 0)),
        scratch_shapes=[pltpu.VMEM((ts + 2 * HALO_ROWS, cw), F32),
                        pltpu.VMEM((7, ts + 2 * HALO_ROWS - 8, cw), F32),
                        pltpu.VMEM((ts, cw), F32)],
        compiler_params=_params("arbitrary", "arbitrary"),
        name="conv_module",
    )(u, u, u, u, u, u, w_dw, b_dw.reshape(1, cw), g_ln.reshape(1, cw), b_ln.reshape(1, cw))


def _cross_attn_kernel(q_ref, k_ref, v_ref, o_ref, *, head_dim):
    scale = head_dim ** -0.5
    for h in range(N_CROSS_HEADS):
        cols = slice(h * head_dim, (h + 1) * head_dim)
        s = lax.dot_general(q_ref[:, cols], k_ref[:, cols], (((1,), (1,)), ((), ())),
                            preferred_element_type=F32) * scale
        e = jnp.exp(s - jnp.max(s, axis=-1, keepdims=True))
        p = e * (1.0 / jnp.sum(e, axis=-1, keepdims=True))
        o_ref[:, cols] = jnp.dot(p.astype(BF16), v_ref[:, cols],
                                 preferred_element_type=F32).astype(o_ref.dtype)


def _cross_attention(q, k, v, *, batch, seq, n_mem, tm):
    width = q.shape[1]
    qb = seq // tm
    return pl.pallas_call(
        functools.partial(_cross_attn_kernel, head_dim=width // N_CROSS_HEADS),
        out_shape=jax.ShapeDtypeStruct(q.shape, BF16),
        grid=(batch, qb),
        in_specs=[pl.BlockSpec((tm, width), lambda b, i: (b * qb + i, 0)),
                  pl.BlockSpec((n_mem, width), lambda b, i: (b, 0)),
                  pl.BlockSpec((n_mem, width), lambda b, i: (b, 0))],
        out_specs=pl.BlockSpec((tm, width), lambda b, i: (b * qb + i, 0)),
        compiler_params=_params("arbitrary", "arbitrary"),
        name="cross_attention",
    )(q, k, v)


def _pack_bf16_pairs(lo_half, hi_half):
    a = pltpu.bitcast(lo_half, U32)
    b = pltpu.bitcast(hi_half, U32)
    return (a & jnp.uint32(0xFFFF0000)) | (b >> 16)


def _unpack_bf16_pairs(words):
    a = pltpu.bitcast(words & jnp.uint32(0xFFFF0000), F32)
    b = pltpu.bitcast(words << 16, F32)
    return a.astype(BF16), b.astype(BF16)


def _router_kernel(x_ref, g_ref, w_ref, b_ref, hp_ref, idx_ref, gate_ref, rank_ref, cnt_ref,
                   seen_ref):
    @pl.when(pl.program_id(0) == 0)
    def _():
        seen_ref[...] = jnp.zeros(seen_ref.shape, F32)

    x = x_ref[...]
    tm, d = x.shape
    y = x * lax.rsqrt(jnp.mean(x * x, axis=-1, keepdims=True) + EPS)
    hb = (y * g_ref[...]).astype(BF16)
    hp_ref[...] = _pack_bf16_pairs(hb[:, :d // 2].astype(F32), hb[:, d // 2:].astype(F32))
    logits = jnp.dot(hb, w_ref[...].astype(BF16), preferred_element_type=F32) + b_ref[...]
    n_exp = logits.shape[1]
    lane = lax.broadcasted_iota(jnp.int32, logits.shape, 1)
    kl = lax.broadcasted_iota(jnp.int32, idx_ref.shape, 1)
    vals = logits
    top_v = jnp.zeros(gate_ref.shape, F32)
    top_i = jnp.zeros(idx_ref.shape, jnp.int32)
    picks = []
    for kk in range(TOP_K):
        mx = jnp.max(vals, axis=-1, keepdims=True)
        sel = jnp.min(jnp.where(vals == mx, lane, n_exp), axis=-1, keepdims=True)
        top_v = jnp.where(kl == kk, mx, top_v)
        top_i = jnp.where(kl == kk, sel, top_i)
        picks.append(lane == sel)
        vals = jnp.where(picks[-1], -jnp.inf, vals)
    e = jnp.exp(top_v - jnp.max(top_v, axis=-1, keepdims=True))
    gate_ref[...] = e / jnp.sum(e, axis=-1, keepdims=True)
    idx_ref[...] = top_i
    member = jnp.zeros(logits.shape, F32)
    for pick in picks:
        member = member + pick.astype(F32)
    earlier = (lax.broadcasted_iota(jnp.int32, (tm, tm), 0)
               > lax.broadcasted_iota(jnp.int32, (tm, tm), 1)).astype(BF16)
    before = seen_ref[...] + jnp.dot(earlier, member.astype(BF16), preferred_element_type=F32)
    rank = jnp.zeros(rank_ref.shape, F32)
    for kk, pick in enumerate(picks):
        rank = jnp.where(kl == kk, jnp.sum(jnp.where(pick, before, 0.0), axis=-1, keepdims=True),
                         rank)
    rank_ref[...] = rank.astype(jnp.int32)
    seen_ref[...] = seen_ref[...] + jnp.sum(member, axis=0, keepdims=True)
    cnt_ref[...] = seen_ref[...].astype(jnp.int32)


def _router(x, g, w_router, b_router, layer, *, tm):
    n, d = x.shape
    n_exp = w_router.shape[2]
    per_tok = pl.BlockSpec((tm, TOP_K), lambda i: (i, 0))
    return pl.pallas_call(
        _router_kernel,
        out_shape=(jax.ShapeDtypeStruct((n, d // 2), U32),
                   jax.ShapeDtypeStruct((n, TOP_K), jnp.int32),
                   jax.ShapeDtypeStruct((n, TOP_K), F32),
                   jax.ShapeDtypeStruct((n, TOP_K), jnp.int32),
                   jax.ShapeDtypeStruct((1, n_exp), jnp.int32)),
        grid=(n // tm,),
        in_specs=[pl.BlockSpec((tm, d), lambda i: (i, 0)),
                  pl.BlockSpec((1, d), lambda i: (0, 0)),
                  pl.BlockSpec((None, d, n_exp), lambda i: (layer, 0, 0)),
                  pl.BlockSpec((1, n_exp), lambda i: (0, 0))],
        out_specs=(pl.BlockSpec((tm, d // 2), lambda i: (i, 0)), per_tok, per_tok, per_tok,
                   pl.BlockSpec((1, n_exp), lambda i: (0, 0))),
        scratch_shapes=[pltpu.VMEM((1, n_exp), F32)],
        compiler_params=_params("arbitrary"),
        name="router",
    )(x, g.reshape(1, d), w_router, b_router.reshape(1, n_exp))


def _dispatch_kernel(zt_ref, dest_ref, src_ref, dst_ref, zero_ref, zsem, sem, *, tt, tm):
    @pl.when(pl.program_id(0) == 0)
    def _():
        zero_ref[...] = jnp.zeros(zero_ref.shape, zero_ref.dtype)

        def zero_copy(i):
            return pltpu.make_async_copy(zero_ref, dst_ref.at[pl.ds(zt_ref[i] * tm, tm)], zsem.at[0])

        def start(i, carry):
            @pl.when(zt_ref[i] >= 0)
            def _():
                zero_copy(i).start()
            return carry

        def wait(i, carry):
            @pl.when(zt_ref[i] >= 0)
            def _():
                zero_copy(i).wait()
            return carry

        lax.fori_loop(0, zt_ref.shape[0], start, 0)
        lax.fori_loop(0, zt_ref.shape[0], wait, 0)

    def row_copy(r, kk):
        return pltpu.make_async_copy(src_ref.at[pl.ds(r, 1)],
                                     dst_ref.at[pl.ds(dest_ref[0, 0, r * TOP_K + kk], 1)],
                                     sem.at[0])

    for r in range(tt):
        for kk in range(TOP_K):
            row_copy(r, kk).start(priority=kk % 2)
    for kk in range(TOP_K):
        pltpu.make_async_copy(src_ref, dst_ref.at[pl.ds(0, tt)], sem.at[0]).wait()


def _dispatch(src, dest, zero_tiles, *, p, tt, tm):
    n, half = src.shape
    nt = n // tt
    return pl.pallas_call(
        functools.partial(_dispatch_kernel, tt=tt, tm=tm),
        out_shape=jax.ShapeDtypeStruct((p, half), src.dtype),
        grid_spec=pltpu.PrefetchScalarGridSpec(
            num_scalar_prefetch=1,
            grid=(nt,),
            in_specs=[pl.BlockSpec((1, 1, tt * TOP_K), lambda t, zt: (t, 0, 0),
                                   memory_space=pltpu.SMEM),
                      pl.BlockSpec((tt, half), lambda t, zt: (t, 0))],
            out_specs=pl.BlockSpec(memory_space=pl.ANY),
            scratch_shapes=[pltpu.VMEM((tm, half), src.dtype),
                            pltpu.SemaphoreType.DMA((1,)), pltpu.SemaphoreType.DMA((1,))],
        ),
        compiler_params=_params("arbitrary"),
        name="moe_dispatch",
    )(zero_tiles, dest.reshape(nt, 1, tt * TOP_K), src)


def _is_first_tile_of_expert(te_ref, t):
    return jnp.logical_or(t == 0, te_ref[t] != te_ref[jnp.maximum(t - 1, 0)])


def _gmm1_kernel(te_ref, nu_ref, nx_ref, x_ref, w_ref, ba_ref, bb_ref, o_ref,
                 stage_a, stage_b, wa_bf, wb_bf, sem, *, layer, tf, nj):
    j = pl.program_id(0)
    t = pl.program_id(1)

    def fetch(e, jj):
        cols_a = pl.ds(pl.multiple_of(jj * tf, tf), tf)
        cols_b = pl.ds(pl.multiple_of((nj + jj) * tf, tf), tf)
        return (pltpu.make_async_copy(w_ref.at[layer, e, :, cols_a], stage_a, sem.at[0]),
                pltpu.make_async_copy(w_ref.at[layer, e, :, cols_b], stage_b, sem.at[1]))

    @pl.when(t < nu_ref[0])
    def _():
        @pl.when(_is_first_tile_of_expert(te_ref, t))
        def _():
            @pl.when(jnp.logical_and(j == 0, t == 0))
            def _():
                for copy in fetch(te_ref[0], 0):
                    copy.start()

            for copy in fetch(te_ref[t], j):
                copy.wait()
            wa_bf[...] = stage_a[...].astype(BF16)
            wb_bf[...] = stage_b[...].astype(BF16)
            more_here = nx_ref[t] >= 0
            next_e = jnp.where(more_here, nx_ref[t], te_ref[0])
            next_j = jnp.where(more_here, j, j + 1)

            @pl.when(next_j < nj)
            def _():
                for copy in fetch(next_e, next_j):
                    copy.start()

        x_lo, x_hi = _unpack_bf16_pairs(x_ref[...])
        half = x_lo.shape[1]

        def proj(w_bf, b_ref):
            return (jnp.dot(x_lo, w_bf[:half, :], preferred_element_type=F32)
                    + jnp.dot(x_hi, w_bf[half:, :], preferred_element_type=F32) + b_ref[...])

        a = jnp.minimum(proj(wa_bf, ba_ref), SWIGLU_LIMIT)
        b = jnp.clip(proj(wb_bf, bb_ref), -SWIGLU_LIMIT, SWIGLU_LIMIT)
        o_ref[...] = (a * jax.nn.sigmoid(SWIGLU_ALPHA * a) * (b + 1.0)).astype(o_ref.dtype)


def _gmm1(xg, w1, b1, layer, tile_e, n_used, next_e, *, tm, tf):
    p, half = xg.shape
    d = 2 * half
    n_exp, ff = w1.shape[1], w1.shape[3] // 2
    nj = ff // tf
    assert ff % tf == 0 and p % tm == 0

    def row(t, nu):
        return jnp.minimum(t, nu[0] - 1)

    return pl.pallas_call(
        functools.partial(_gmm1_kernel, layer=layer, tf=tf, nj=nj),
        out_shape=jax.ShapeDtypeStruct((p, ff), BF16),
        grid_spec=pltpu.PrefetchScalarGridSpec(
            num_scalar_prefetch=3,
            grid=(nj, p // tm),
            in_specs=[
                pl.BlockSpec((tm, half), lambda j, t, te, nu, nx: (row(t, nu), 0)),
                pl.BlockSpec(memory_space=pl.ANY),
                pl.BlockSpec((None, None, 1, tf), lambda j, t, te, nu, nx: (layer, te[t], 0, j)),
                pl.BlockSpec((None, None, 1, tf),
                             lambda j, t, te, nu, nx: (layer, te[t], 0, nj + j)),
            ],
            out_specs=pl.BlockSpec((tm, tf), lambda j, t, te, nu, nx: (row(t, nu), j)),
            scratch_shapes=[pltpu.VMEM((d, tf), F32), pltpu.VMEM((d, tf), F32),
                            pltpu.VMEM((d, tf), BF16), pltpu.VMEM((d, tf), BF16),
                            pltpu.SemaphoreType.DMA((2,))],
        ),
        compiler_params=_params("arbitrary", "arbitrary"),
        name="moe_gmm1",
    )(tile_e, n_used, next_e, xg, w1, b1.reshape(b1.shape[0], n_exp, 1, 2 * ff),
      b1.reshape(b1.shape[0], n_exp, 1, 2 * ff))


def _gmm2_kernel(te_ref, nu_ref, nx_ref, x_ref, w_ref, b_ref, o_ref, stage, w_bf, sem, *,
                 layer, tn, nj):
    j = pl.program_id(0)
    t = pl.program_id(1)

    def fetch(e, jj):
        cols = pl.ds(pl.multiple_of(jj * tn, tn), tn)
        return pltpu.make_async_copy(w_ref.at[layer, e, :, cols], stage, sem.at[0])

    @pl.when(t < nu_ref[0])
    def _():
        @pl.when(_is_first_tile_of_expert(te_ref, t))
        def _():
            @pl.when(jnp.logical_and(j == 0, t == 0))
            def _():
                fetch(te_ref[0], 0).start()

            fetch(te_ref[t], j).wait()
            w_bf[...] = stage[...].astype(BF16)
            more_here = nx_ref[t] >= 0
            next_e = jnp.where(more_here, nx_ref[t], te_ref[0])
            next_j = jnp.where(more_here, j, j + 1)

            @pl.when(next_j < nj)
            def _():
                fetch(next_e, next_j).start()

        y = jnp.dot(x_ref[...], w_bf[...], preferred_element_type=F32) + b_ref[...]
        yb = y.astype(BF16).astype(F32)
        half = yb.shape[1] // 2
        o_ref[...] = _pack_bf16_pairs(yb[:, :half], yb[:, half:])


def _gmm2(act, w2, b2, layer, tile_e, n_used, next_e, *, tm, tn):
    p, ff = act.shape
    n_exp, d = w2.shape[1], w2.shape[3]
    assert d % tn == 0
    nj = d // tn

    def row(t, nu):
        return jnp.minimum(t, nu[0] - 1)

    return pl.pallas_call(
        functools.partial(_gmm2_kernel, layer=layer, tn=tn, nj=nj),
        out_shape=jax.ShapeDtypeStruct((p, d // 2), U32),
        grid_spec=pltpu.PrefetchScalarGridSpec(
            num_scalar_prefetch=3,
            grid=(nj, p // tm),
            in_specs=[
                pl.BlockSpec((tm, ff), lambda j, t, te, nu, nx: (row(t, nu), 0)),
                pl.BlockSpec(memory_space=pl.ANY),
                pl.BlockSpec((None, None, 1, tn), lambda j, t, te, nu, nx: (layer, te[t], 0, j)),
            ],
            out_specs=pl.BlockSpec((tm, tn // 2), lambda j, t, te, nu, nx: (row(t, nu), j)),
            scratch_shapes=[pltpu.VMEM((ff, tn), F32), pltpu.VMEM((ff, tn), BF16),
                            pltpu.SemaphoreType.DMA((1,))],
        ),
        compiler_params=_params("arbitrary", "arbitrary"),
        name="moe_gmm2",
    )(tile_e, n_used, next_e, act, w2, b2.reshape(b2.shape[0], n_exp, 1, d))


def _unpack_expert_rows(words, pack_cols):
    lo = pltpu.bitcast(words & jnp.uint32(0xFFFF0000), F32)
    hi = pltpu.bitcast(words << 16, F32)
    half = pack_cols // 2
    pieces = []
    for c in range(0, words.shape[1], half):
        pieces += [lo[:, c:c + half], hi[:, c:c + half]]
    return jnp.concatenate(pieces, axis=1)


def _combine_kernel(d0_ref, d1_ref, d2_ref, x_ref, gate_ref, g_ref, y_ref, o_ref, buf, sem, *,
                    tt, pack_cols, final_norm):
    t = pl.program_id(0)
    last = pl.num_programs(0) - 1
    slot = lax.rem(t, COMBINE_SLOTS)
    ahead = lax.rem(t + 2, COMBINE_SLOTS)

    def row_copy(d_ref, r, kk, s):
        return pltpu.make_async_copy(y_ref.at[pl.ds(d_ref[0, 0, r * TOP_K + kk], 1)],
                                     buf.at[s, kk, pl.ds(r, 1)], sem.at[s])

    def issue(d_ref, s, rows):
        for r in rows:
            for kk in range(TOP_K):
                row_copy(d_ref, r, kk, s).start(priority=kk % 2)

    def drain(s):
        for kk in range(TOP_K):
            pltpu.make_async_copy(y_ref.at[pl.ds(0, tt)], buf.at[s, kk], sem.at[s]).wait()

    @pl.when(t == 0)
    def _():
        issue(d0_ref, 0, range(tt))
        issue(d1_ref, 1, range(tt))

    drain(slot)
    gate = gate_ref[...]
    for r0 in range(0, tt, COMBINE_GROUP):
        rows = slice(r0, r0 + COMBINE_GROUP)
        acc = x_ref[rows, :]
        parts = [_unpack_expert_rows(buf[slot, kk, rows, :], pack_cols) for kk in range(TOP_K)]
        issue(d2_ref, ahead, range(r0, r0 + COMBINE_GROUP))
        for kk in range(TOP_K):
            acc = acc + gate[rows, kk:kk + 1] * parts[kk]
        if final_norm:
            acc = acc * lax.rsqrt(jnp.mean(acc * acc, axis=-1, keepdims=True) + EPS) * g_ref[...]
        o_ref[rows, :] = acc

    @pl.when(t == last)
    def _():
        drain(lax.rem(t + 1, COMBINE_SLOTS))
        drain(ahead)


def _combine(x, y, dest, gate, g_final, *, tt, pack_cols, final_norm):
    n, d = x.shape
    nt = n // tt
    assert nt >= COMBINE_SLOTS and tt % COMBINE_GROUP == 0
    dest3 = dest.reshape(nt, 1, tt * TOP_K)

    def tile(k):
        return pl.BlockSpec((1, 1, tt * TOP_K), lambda t: (jnp.minimum(t + k, nt - 1), 0, 0),
                            memory_space=pltpu.SMEM)

    return pl.pallas_call(
        functools.partial(_combine_kernel, tt=tt, pack_cols=pack_cols, final_norm=final_norm),
        out_shape=jax.ShapeDtypeStruct((n, d), F32),
        grid=(nt,),
        in_specs=[
            tile(0), tile(1), tile(2),
            pl.BlockSpec((tt, d), lambda t: (t, 0)),
            pl.BlockSpec((tt, TOP_K), lambda t: (t, 0)),
            pl.BlockSpec((1, d), lambda t: (0, 0)),
            pl.BlockSpec(memory_space=pl.ANY),
        ],
        out_specs=pl.BlockSpec((tt, d), lambda t: (t, 0)),
        scratch_shapes=[pltpu.VMEM((COMBINE_SLOTS, TOP_K, tt, d // 2), U32),
                        pltpu.SemaphoreType.DMA((COMBINE_SLOTS,))],
        compiler_params=_params("arbitrary"),
        name="moe_combine",
    )(dest3, dest3, dest3, x, gate, g_final.reshape(1, d), y)


def _routing_tables(top_idx, rank, counts, tm):
    n = top_idx.shape[0]
    n_exp = counts.shape[0]
    n_tiles = (n * TOP_K) // tm + n_exp
    padded = (counts + tm - 1) // tm * tm
    pad_end = jnp.cumsum(padded)
    pad_start = pad_end - padded
    experts = jnp.arange(n_exp, dtype=jnp.int32)
    dest = rank + jnp.sum(jnp.where(top_idx[..., None] == experts, pad_start, 0), axis=-1)
    tile_start = jnp.arange(n_tiles, dtype=jnp.int32) * tm
    tile_e = jnp.minimum(jnp.sum(tile_start[:, None] >= pad_end[None, :], axis=-1),
                         n_exp - 1).astype(jnp.int32)
    n_used = pad_end[-1:] // tm
    later = jnp.logical_and(experts[None, :] > tile_e[:, None], counts[None, :] > 0)
    next_e = jnp.min(jnp.where(later, experts[None, :], n_exp), axis=-1)
    next_e = jnp.where(next_e < n_exp, next_e, -1)
    ragged = jnp.where(counts % tm != 0, pad_end // tm - 1, -1)
    tail = n_used + experts
    zero_tiles = jnp.concatenate([ragged, jnp.where(tail < n_tiles, tail, -1)])
    return (dest.astype(jnp.int32), tile_e, n_used.astype(jnp.int32), next_e.astype(jnp.int32),
            zero_tiles.astype(jnp.int32), n_tiles * tm)


def _pick(n, pref):
    t = min(n, pref)
    while n % t or t % 8:
        t -= 1
    return t


def kernel(x, mem, g_mix, w_in, lambda_q1, lambda_k1, lambda_q2, lambda_k2, g_subln, w_dw, b_dw,
           g_conv_ln, b_conv_ln, w_out, rel_bias, g_cross, g_mem, w_cq, w_ck, w_cv, w_co, g_ffn,
           w_router, b_router, w1, b1, w2, b2, g_final):
    batch, seq, d = x.shape
    n = batch * seq
    n_mem = mem.shape[1]
    depth = g_mix.shape[0]
    n_heads = rel_bias.shape[1]
    attn_w = n_heads * 2 * DIFF_HEAD_DIM
    conv_w = w_dw.shape[2]
    in_w = w_in.shape[2]
    n_exp = w_router.shape[2]
    assert in_w == 3 * attn_w + 2 * conv_w and (3 * attn_w) % conv_w == 0

    tq = _pick(seq, 512)
    xf = x.reshape(n, d)
    memf = mem.reshape(batch * n_mem, d)
    for l in range(depth):
        lambda_init = 0.8 - 0.6 * math.exp(-0.3 * l)
        tbl, lam = _attn_prep(rel_bias, lambda_q1[l], lambda_k1[l], lambda_q2[l], lambda_k2[l],
                              tb=tq, lambda_init=lambda_init)
        h = _rmsnorm(xf, g_mix[l], rows=_pick(n, 256))
        u = _dense([h], w_in, l, tm=_pick(n, 512), tn=_pick(in_w, 1024), out_dtype=BF16,
                   name="in_proj").reshape(batch, seq, in_w)
        attn = _diff_attention(u, tbl, lam, g_subln[l], batch=batch, seq=seq, n_heads=n_heads,
                               tq=tq, tk=_pick(seq, 512), lambda_init=lambda_init)
        conv = _conv_module(u, w_dw[l], b_dw[l], g_conv_ln[l], b_conv_ln[l], batch=batch, seq=seq,
                            col_block=3 * attn_w // conv_w, ts=_pick(seq, 256))
        xf = _dense([attn.reshape(n, attn_w), conv.reshape(n, conv_w)], w_out, l,
                    tm=_pick(n, 1024), tn=_pick(d, 512), out_dtype=F32, res=xf, name="out_proj")
        m = _rmsnorm(memf, g_mem[l], rows=_pick(batch * n_mem, 256))
        cross_w = w_cq.shape[2]
        qc = _dense([xf], w_cq, l, tm=_pick(n, 256), tn=_pick(cross_w, 1024), out_dtype=BF16,
                    norm_g=g_cross[l], name="cross_q")
        kc = _dense([m], w_ck, l, tm=_pick(batch * n_mem, 512), tn=_pick(cross_w, 512),
                    out_dtype=BF16, name="cross_k")
        vc = _dense([m], w_cv, l, tm=_pick(batch * n_mem, 512), tn=_pick(cross_w, 512),
                    out_dtype=BF16, name="cross_v")
        oc = _cross_attention(qc, kc, vc, batch=batch, seq=seq, n_mem=n_mem, tm=_pick(seq, 512))
        xf = _dense([oc], w_co, l, tm=_pick(n, 512), tn=_pick(d, 2048), out_dtype=F32, res=xf,
                    name="cross_o")
        tm = 512
        hp, top_idx, gate, rank, counts = _router(xf, g_ffn[l], w_router, b_router[l], l,
                                                  tm=_pick(n, 256))
        dest, tile_e, n_used, next_e, zero_tiles, p_rows = _routing_tables(
            top_idx, rank, counts[0], tm)
        xg = _dispatch(hp, dest, zero_tiles, p=p_rows, tt=_pick(n, 128), tm=tm)
        ff = w2.shape[2]
        act = _gmm1(xg, w1, b1, l, tile_e, n_used, next_e, tm=tm, tf=_pick(ff, 512))
        tn = _pick(d, 2048)
        y = _gmm2(act, w2, b2, l, tile_e, n_used, next_e, tm=tm, tn=tn)
        xf = _combine(xf, y, dest, gate, g_final, tt=_pick(n, 128), pack_cols=tn,
                      final_norm=l == depth - 1)
    return xf.reshape(batch, seq, d)
```
